```python
import jax, jax.numpy as jnp
from jax import lax
import numpy as np

D_MODEL = 1024
BATCH = 16
SEQ = 4096
DEPTH = 4
DEC_BATCH = 32
DEC_SEQ = 2048
PAST_LEN = 128

GRID_W = 64
HEAD_DIM = 64
NA_HEADS = 8
NA_WIDTH = NA_HEADS * HEAD_DIM
NA_WIN_H_MAX = 8
NA_WIN_W = 16
GQA_Q_HEADS = 8
GQA_KV_HEADS = 2
GQA_Q_WIDTH = GQA_Q_HEADS * HEAD_DIM
GQA_KV_WIDTH = GQA_KV_HEADS * HEAD_DIM
ROPE_AXIS_DIM = HEAD_DIM // 2
ROPE_THETA = 10000.0
Q_BLOCK = 128
EPS = 1e-6
IN_SPLITS = (NA_WIDTH, NA_WIDTH, NA_WIDTH, NA_WIDTH,
             GQA_Q_WIDTH, GQA_KV_WIDTH, GQA_KV_WIDTH, GQA_Q_WIDTH,
             D_MODEL, D_MODEL)
D_IN = sum(IN_SPLITS)

kernel_name = "hybrid_natten_gqa_axial_encoder"


def _rms_norm(x, g):
    xf = x.astype(jnp.float32)
    y = xf * lax.rsqrt(jnp.mean(xf * xf, axis=-1, keepdims=True) + EPS)
    return (y * g.astype(jnp.float32)).astype(x.dtype)


def _axial_angles(n):
    t = jnp.arange(n)
    row = (t // GRID_W).astype(jnp.float32)
    col = (t % GRID_W).astype(jnp.float32)
    inv = ROPE_THETA ** (-jnp.arange(0, ROPE_AXIS_DIM, 2, dtype=jnp.float32) / ROPE_AXIS_DIM)
    return row[:, None] * inv, col[:, None] * inv


def _rotate_half(xf, ang):
    h = xf.shape[-1] // 2
    x1, x2 = xf[..., :h], xf[..., h:]
    cos = jnp.cos(ang)[None, :, None, :]
    sin = jnp.sin(ang)[None, :, None, :]
    return jnp.concatenate([x1 * cos - x2 * sin, x1 * sin + x2 * cos], axis=-1)


def _axial_rope(x):
    n = x.shape[1]
    ang_r, ang_c = _axial_angles(n)
    xf = x.astype(jnp.float32)
    out = jnp.concatenate([_rotate_half(xf[..., :ROPE_AXIS_DIM], ang_r),
                           _rotate_half(xf[..., ROPE_AXIS_DIM:], ang_c)], axis=-1)
    return out.astype(x.dtype)


def _neighbourhood_attention(q, k, v, rpb):
    b, n, h, dh = q.shape
    rows = n // GRID_W
    kh = min(NA_WIN_H_MAX, rows)
    kw = NA_WIN_W
    q = q.reshape(b, rows, GRID_W, h, dh)
    k = k.reshape(b, rows, GRID_W, h, dh)
    v = v.reshape(b, rows, GRID_W, h, dh)
    cols = np.arange(GRID_W)
    col_start = np.clip(cols - kw // 2, 0, GRID_W - kw)
    col_idx = col_start[:, None] + np.arange(kw)[None, :]
    col_off = col_idx - cols[:, None] + (kw - 1)
    rpb_cols = rpb[:, :, col_off].astype(jnp.float32)
    scale = HEAD_DIM ** -0.5

    def row_block(r):
        r0 = jnp.clip(r - kh // 2, 0, rows - kh)
        q_r = lax.dynamic_index_in_dim(q, r, axis=1, keepdims=False)
        k_band = lax.dynamic_slice_in_dim(k, r0, kh, axis=1)
        v_band = lax.dynamic_slice_in_dim(v, r0, kh, axis=1)
        k_g = k_band[:, :, col_idx]
        v_g = v_band[:, :, col_idx]
        s = jnp.einsum('bqhd,baqchd->bhqac', q_r, k_g).astype(jnp.float32) * scale
        row_off = r0 + jnp.arange(kh) - r + (NA_WIN_H_MAX - 1)
        bias = jnp.take(rpb_cols, row_off, axis=1)
        s = s + jnp.transpose(bias, (0, 2, 1, 3))[None]
        p = jax.nn.softmax(s.reshape(b, h, GRID_W, kh * kw), axis=-1).reshape(s.shape)
        return jnp.einsum('bhqac,baqchd->bqhd', p.astype(v.dtype), v_g)

    out = lax.map(row_block, jnp.arange(rows))
    return jnp.transpose(out, (1, 0, 2, 3, 4)).reshape(b, n, h * dh)


def _gqa_attention(q, k, v):
    b, n, hq, dh = q.shape
    hkv = k.shape[2]
    g = hq // hkv
    nb = n // Q_BLOCK
    qb = jnp.transpose(q.reshape(b, nb, Q_BLOCK, hkv, g, dh), (1, 0, 2, 3, 4, 5))
    scale = HEAD_DIM ** -0.5

    def block(qblk):
        s = jnp.einsum('bqkgd,bnkd->bkgqn', qblk, k).astype(jnp.float32) * scale
        p = jax.nn.softmax(s, axis=-1)
        return jnp.einsum('bkgqn,bnkd->bqkgd', p.astype(v.dtype), v)

    out = lax.map(block, qb)
    return jnp.transpose(out, (1, 0, 2, 3, 4, 5)).reshape(b, n, hq * dh)


def _layer(x, c, norm_g, w_ada, b_ada, w_in, b_gate, rpb, q_norm_g, k_norm_g, w_pa, w_pb, w_out):
    b, n, _ = x.shape
    mod = jax.nn.silu(c) @ w_ada + b_ada
    shift, scale, gate = jnp.split(mod, 3, axis=-1)
    hid = _rms_norm(x, norm_g) * (1.0 + scale[:, None, :]) + shift[:, None, :]
    proj = hid @ w_in
    split_at = np.cumsum(IN_SPLITS)[:-1].tolist()
    qa, ka, va, za, qb, kb, vb, zb, ga, gb = jnp.split(proj, split_at, axis=-1)
    oa = _neighbourhood_attention(qa.reshape(b, n, NA_HEADS, HEAD_DIM),
                                  ka.reshape(b, n, NA_HEADS, HEAD_DIM),
                                  va.reshape(b, n, NA_HEADS, HEAD_DIM), rpb)
    ya = (oa * jax.nn.silu(za)) @ w_pa
    qh = _axial_rope(_rms_norm(qb.reshape(b, n, GQA_Q_HEADS, HEAD_DIM), q_norm_g))
    kh = _axial_rope(_rms_norm(kb.reshape(b, n, GQA_KV_HEADS, HEAD_DIM), k_norm_g))
    ob = _gqa_attention(qh, kh, vb.reshape(b, n, GQA_KV_HEADS, HEAD_DIM))
    yb = (ob * jax.nn.silu(zb)) @ w_pb
    g_all = jnp.concatenate([ga, gb], axis=-1) + b_gate
    g_a, g_b = jnp.split(jax.nn.sigmoid(g_all), 2, axis=-1)
    merged = g_a * ya + g_b * yb
    return x + gate[:, None, :] * (merged @ w_out)


def _trunk(x, c, norm_g, w_ada, b_ada, w_in, b_gate, rpb, q_norm_g, k_norm_g, w_pa, w_pb, w_out, final_g):
    for l in range(DEPTH):
        x = _layer(x, c, norm_g[l], w_ada[l], b_ada[l], w_in[l], b_gate[l], rpb[l],
                   q_norm_g[l], k_norm_g[l], w_pa[l], w_pb[l], w_out[l])
    return _rms_norm(x, final_g)


def setup_inputs(seed: int = 0) -> dict:
    key = jax.random.key(seed)
    ks = jax.random.split(key, 16)
    f32 = jnp.float32
    d = D_MODEL
    nrm = lambda k, shape: jax.random.normal(k, shape, dtype=f32)
    return {
        "x_prompt": nrm(ks[0], (BATCH, SEQ, d)),
        "x_sample": nrm(ks[1], (DEC_BATCH, DEC_SEQ, d)),
        "c_prompt": nrm(ks[2], (BATCH, d)),
        "c_sample": nrm(ks[3], (DEC_BATCH, d)),
        "norm_g": 1.0 + 0.02 * nrm(ks[4], (DEPTH, d)),
        "w_ada": nrm(ks[5], (DEPTH, d, 3 * d)) * (0.5 * d ** -0.5),
        "b_ada": 0.01 * nrm(ks[6], (DEPTH, 3 * d)),
        "w_in": nrm(ks[7], (DEPTH, d, D_IN)) * d ** -0.5,
        "b_gate": 0.01 * nrm(ks[8], (DEPTH, 2 * d)),
        "rpb": 0.02 * nrm(ks[9], (DEPTH, NA_HEADS, 2 * NA_WIN_H_MAX - 1, 2 * NA_WIN_W - 1)),
        "q_norm_g": 1.0 + 0.02 * nrm(ks[10], (DEPTH, HEAD_DIM)),
        "k_norm_g": 1.0 + 0.02 * nrm(ks[11], (DEPTH, HEAD_DIM)),
        "w_pa": nrm(ks[12], (DEPTH, NA_WIDTH, d)) * NA_WIDTH ** -0.5,
        "w_pb": nrm(ks[13], (DEPTH, GQA_Q_WIDTH, d)) * GQA_Q_WIDTH ** -0.5,
        "w_out": nrm(ks[14], (DEPTH, d, d)) * d ** -0.5,
        "final_g": 1.0 + 0.02 * nrm(ks[15], (d,)),
    }


def reference(x_prompt, x_sample, c_prompt, c_sample, norm_g, w_ada, b_ada, w_in, b_gate, rpb,
              q_norm_g, k_norm_g, w_pa, w_pb, w_out, final_g):
    y_prompt = _trunk(x_prompt, c_prompt, norm_g, w_ada, b_ada, w_in, b_gate, rpb,
                      q_norm_g, k_norm_g, w_pa, w_pb, w_out, final_g)
    y_sample = _trunk(x_sample, c_sample, norm_g, w_ada, b_ada, w_in, b_gate, rpb,
                      q_norm_g, k_norm_g, w_pa, w_pb, w_out, final_g)
    return (y_prompt, y_sample)
```

```python
import functools

import jax
import jax.numpy as jnp
import numpy as np
from jax import lax
from jax.experimental import pallas as pl
from jax.experimental.pallas import tpu as pltpu

D_MODEL = 1024
DEPTH = 4
GRID_W = 64
HEAD_DIM = 64
NA_HEADS = 8
NA_WIDTH = NA_HEADS * HEAD_DIM
NA_WIN_H = 8
NA_WIN_W = 16
GQA_Q_HEADS = 8
GQA_KV_HEADS = 2
GQA_GROUP = GQA_Q_HEADS // GQA_KV_HEADS
GQA_Q_WIDTH = GQA_Q_HEADS * HEAD_DIM
GQA_KV_WIDTH = GQA_KV_HEADS * HEAD_DIM
ROPE_AXIS_DIM = HEAD_DIM // 2
ROPE_THETA = 10000.0
EPS = 1e-6
ATTN_SCALE = HEAD_DIM ** -0.5
IN_SPLITS = (NA_WIDTH,) * 4 + (GQA_Q_WIDTH, GQA_KV_WIDTH, GQA_KV_WIDTH, GQA_Q_WIDTH, D_MODEL, D_MODEL)
MASKED = -1e30

LANES = 128
PROJ_WIDTH = 3 * NA_WIDTH + GQA_Q_WIDTH + 2 * GQA_KV_WIDTH
ZG_WIDTH = NA_WIDTH + GQA_Q_WIDTH + 2 * D_MODEL
TOKEN_TILE = 512
NA_ROWS = 4
NA_BAND = 12
GQA_Q_TILE = 256
GQA_K_TILE = 512
VMEM_LIMIT = 56 * 1024 * 1024

_GQA_HEAD_ORDER = (0, 4, 1, 5, 2, 6, 3, 7)


def _low_half(shape):
    return lax.broadcasted_iota(jnp.int32, shape, len(shape) - 1) < HEAD_DIM


def _mod_kernel(c_ref, w_ref, b_ref, o_ref):
    c = c_ref[...]
    sc = c * (1.0 / (1.0 + jnp.exp(-c)))
    o_ref[0] = jnp.dot(sc, w_ref[0], precision=lax.Precision.HIGHEST,
                       preferred_element_type=jnp.float32) + b_ref[0]


def _modulation(c_all, w_ada, b_ada):
    nb = c_all.shape[0]
    n_chunks = 3
    return pl.pallas_call(
        _mod_kernel,
        grid=(DEPTH, n_chunks),
        in_specs=[
            pl.BlockSpec((nb, D_MODEL), lambda l, j: (0, 0)),
            pl.BlockSpec((1, D_MODEL, D_MODEL), lambda l, j: (l, 0, j)),
            pl.BlockSpec((1, 1, D_MODEL), lambda l, j: (l, 0, j)),
        ],
        out_specs=pl.BlockSpec((1, nb, D_MODEL), lambda l, j: (l, 0, j)),
        out_shape=jax.ShapeDtypeStruct((DEPTH, nb, 3 * D_MODEL), jnp.float32),
        compiler_params=pltpu.CompilerParams(vmem_limit_bytes=VMEM_LIMIT),
        name="adaln_modulation",
    )(c_all, w_ada, b_ada.reshape(DEPTH, 1, 3 * D_MODEL))


def _modulated_norm(x, g, scale, shift):
    ms = jnp.mean(x * x, axis=-1, keepdims=True)
    y = x * lax.rsqrt(ms + EPS) * g
    return y * (1.0 + scale) + shift


def _head_norm_rope(x, g, cos, sin_signed, low):
    sq = x * x
    lo = jnp.sum(jnp.where(low, sq, 0.0), axis=-1, keepdims=True)
    hi = jnp.sum(jnp.where(low, 0.0, sq), axis=-1, keepdims=True)
    ms = jnp.where(low, lo, hi) * (1.0 / HEAD_DIM)
    y = x * lax.rsqrt(ms + EPS) * g
    lane = lax.broadcasted_iota(jnp.int32, x.shape, 1)
    quarter = ROPE_AXIS_DIM // 2
    partner = jnp.where((lane & quarter) != 0, pltpu.roll(y, quarter, 1), pltpu.roll(y, LANES - quarter, 1))
    return y * cos + partner * sin_signed


def _in_proj_kernel(x_ref, g_ref, scale_ref, shift_ref, w_ref, cos_ref, sin_ref, qg_ref, kg_ref, o_ref):
    hid = _modulated_norm(x_ref[...], g_ref[...], scale_ref[0], shift_ref[0])
    acc = jnp.dot(hid.astype(jnp.bfloat16), w_ref[...], preferred_element_type=jnp.float32)
    na_cols = 3 * NA_WIDTH
    o_ref[:, :na_cols] = acc[:, :na_cols].astype(o_ref.dtype)
    cos = cos_ref[...]
    sin = sin_ref[...]
    low = _low_half(cos.shape)
    n_q_chunks = GQA_Q_WIDTH // LANES
    for j in range(n_q_chunks + 1):
        c0 = na_cols + j * LANES
        gain = qg_ref[...] if j < n_q_chunks else kg_ref[...]
        o_ref[:, c0:c0 + LANES] = _head_norm_rope(acc[:, c0:c0 + LANES], gain, cos, sin, low).astype(o_ref.dtype)
    v0 = na_cols + GQA_Q_WIDTH + GQA_KV_WIDTH
    o_ref[:, v0:] = acc[:, v0:].astype(o_ref.dtype)


def _in_projection(x2, norm_g, scale, shift, w1, cos_t, sin_t, qg, kg, seq):
    tokens = x2.shape[0]
    tm = TOKEN_TILE
    per_seq = seq // tm
    vec = lambda i: (0, 0)
    return pl.pallas_call(
        _in_proj_kernel,
        grid=(tokens // tm,),
        in_specs=[
            pl.BlockSpec((tm, D_MODEL), lambda i: (i, 0)),
            pl.BlockSpec((1, D_MODEL), vec),
            pl.BlockSpec((1, 1, D_MODEL), lambda i: (i // per_seq, 0, 0)),
            pl.BlockSpec((1, 1, D_MODEL), lambda i: (i // per_seq, 0, 0)),
            pl.BlockSpec((D_MODEL, PROJ_WIDTH), vec),
            pl.BlockSpec((tm, LANES), lambda i: (i % per_seq, 0)),
            pl.BlockSpec((tm, LANES), lambda i: (i % per_seq, 0)),
            pl.BlockSpec((1, LANES), vec),
            pl.BlockSpec((1, LANES), vec),
        ],
        out_specs=pl.BlockSpec((tm, PROJ_WIDTH), lambda i: (i, 0)),
        out_shape=jax.ShapeDtypeStruct((tokens, PROJ_WIDTH), jnp.bfloat16),
        compiler_params=pltpu.CompilerParams(dimension_semantics=("parallel",), vmem_limit_bytes=VMEM_LIMIT),
        name="in_projection",
    )(x2, norm_g, scale, shift, w1, cos_t, sin_t, qg, kg)


def _na_kernel(q_ref, k_ref, v_ref, bias_ref, o_ref, *, grid_rows):
    g = pl.program_id(1)
    band_row = jnp.clip(NA_ROWS * g - NA_WIN_H // 2, 0, grid_rows - NA_BAND)
    k0 = pl.multiple_of(band_row * GRID_W, NA_ROWS * GRID_W)
    nk = NA_BAND * GRID_W
    nq = NA_ROWS * GRID_W
    low = _low_half((nq, LANES))
    for c in range(NA_WIDTH // LANES):
        cols = slice(c * LANES, (c + 1) * LANES)
        kc = k_ref[0, pl.ds(k0, nk), cols]
        vc = v_ref[0, pl.ds(k0, nk), cols]
        qc = q_ref[0, :, cols].astype(jnp.float32)
        out = jnp.zeros((nq, LANES), jnp.float32)
        for half in range(2):
            own = low if half == 0 else jnp.logical_not(low)
            qm = jnp.where(own, qc, 0.0).astype(jnp.bfloat16)
            s = lax.dot_general(qm, kc, (((1,), (1,)), ((), ())), preferred_element_type=jnp.float32)
            s = s + bias_ref[0, 2 * c + half]
            m = jnp.max(s, axis=-1, keepdims=True)
            p = jnp.exp(s - m)
            l = jnp.sum(p, axis=-1, keepdims=True)
            o = jnp.dot(p.astype(jnp.bfloat16), vc, preferred_element_type=jnp.float32) / l
            out = jnp.where(own, o, out)
        o_ref[0, :, cols] = out.astype(o_ref.dtype)


def _na_attention(proj3, bias):
    nb, seq, _ = proj3.shape
    grid_rows = seq // GRID_W
    assert grid_rows % NA_ROWS == 0 and grid_rows >= NA_BAND
    steps = grid_rows // NA_ROWS
    nq = NA_ROWS * GRID_W
    variant = lambda b, g: ((g > 0).astype(jnp.int32) + (g == steps - 1).astype(jnp.int32), 0, 0, 0)
    return pl.pallas_call(
        functools.partial(_na_kernel, grid_rows=grid_rows),
        grid=(nb, steps),
        in_specs=[
            pl.BlockSpec((1, nq, NA_WIDTH), lambda b, g: (b, g, 0)),
            pl.BlockSpec((1, seq, NA_WIDTH), lambda b, g: (b, 0, 1)),
            pl.BlockSpec((1, seq, NA_WIDTH), lambda b, g: (b, 0, 2)),
            pl.BlockSpec((1, NA_HEADS, nq, NA_BAND * GRID_W), variant),
        ],
        out_specs=pl.BlockSpec((1, nq, NA_WIDTH), lambda b, g: (b, g, 0)),
        out_shape=jax.ShapeDtypeStruct((nb, seq, NA_WIDTH), jnp.bfloat16),
        compiler_params=pltpu.CompilerParams(dimension_semantics=("parallel", "arbitrary"),
                                             vmem_limit_bytes=VMEM_LIMIT),
        name="neighbourhood_attention",
    )(proj3, proj3, proj3, bias)


def _na_bias_tables(rpb_l):
    i = np.arange(NA_ROWS)[:, None]
    a = np.arange(NA_BAND)[None, :]
    d = a - i
    first = np.stack([np.zeros_like(d) - i, np.zeros_like(d), np.zeros_like(d) + 4 - i])
    start = np.stack([np.zeros_like(d), np.zeros_like(d) + i, np.zeros_like(d) + 4])
    del first
    row_ok = (a[None] >= start) & (a[None] < start + NA_WIN_H)
    row_off = np.stack([d + 7, d + 3, d - 1])
    row_off = np.clip(row_off, 0, 2 * NA_WIN_H - 2)
    cq = np.arange(GRID_W)[:, None]
    ck = np.arange(GRID_W)[None, :]
    c0 = np.clip(cq - NA_WIN_W // 2, 0, GRID_W - NA_WIN_W)
    col_ok = (ck >= c0) & (ck < c0 + NA_WIN_W)
    col_off = np.clip(ck - cq + NA_WIN_W - 1, 0, 2 * NA_WIN_W - 2)
    t = rpb_l[:, row_off][:, :, :, :, col_off]
    ok = row_ok[:, :, :, None, None] & col_ok[None, None, None]
    t = jnp.where(ok[None], t.astype(jnp.float32), MASKED)
    t = jnp.transpose(t, (1, 0, 2, 4, 3, 5))
    return t.reshape(3, NA_HEADS, NA_ROWS * GRID_W, NA_BAND * GRID_W)


def _gqa_kernel(q_ref, k_ref, v_ref, o_ref, vt_ref, s_ref, *, seq):
    tq, tk = GQA_Q_TILE, GQA_K_TILE
    n_chunks = seq // tk

    @pl.when(pl.program_id(1) == 0)
    def _():
        ones = jnp.ones((HEAD_DIM, tk), jnp.bfloat16)
        for j in range(n_chunks):
            t = v_ref[0, j * tk:(j + 1) * tk, :].astype(jnp.float32).T
            for hk in range(GQA_KV_HEADS):
                vt_ref[hk, j, :HEAD_DIM, :] = t[hk * HEAD_DIM:(hk + 1) * HEAD_DIM].astype(jnp.bfloat16)
                vt_ref[hk, j, HEAD_DIM:, :] = ones

    low = _low_half((tq, LANES))
    for c in range(GQA_Q_WIDTH // LANES):
        cols = slice(c * LANES, (c + 1) * LANES)
        qc = q_ref[0, :, cols].astype(jnp.float32)
        halves = []
        for hk in range(GQA_KV_HEADS):
            own = low if hk == 0 else jnp.logical_not(low)
            qm = jnp.where(own, qc, 0.0).astype(jnp.bfloat16)

            def scores(j, mx, qm=qm):
                r0 = pl.multiple_of(j * tk, tk)
                s = lax.dot_general(k_ref[0, pl.ds(r0, tk), :], qm, (((1,), (1,)), ((), ())),
                                    preferred_element_type=jnp.float32)
                s_ref[pl.ds(r0, tk), :] = s
                return jnp.maximum(mx, jnp.max(s.reshape(tk // 8, 8, tq), axis=0))

            mx = lax.fori_loop(0, n_chunks, scores, jnp.full((8, tq), -jnp.inf, jnp.float32))
            m = jnp.max(mx, axis=0, keepdims=True)

            def weighted(j, acc, m=m, hk=hk):
                r0 = pl.multiple_of(j * tk, tk)
                p = jnp.exp(s_ref[pl.ds(r0, tk), :] - m).astype(jnp.bfloat16)
                return acc + jnp.dot(vt_ref[hk, j], p, preferred_element_type=jnp.float32)

            acc = lax.fori_loop(0, n_chunks, weighted, jnp.zeros((LANES, tq), jnp.float32))
            halves.append(acc[:HEAD_DIM] / acc[HEAD_DIM:HEAD_DIM + 1])
        o_ref[0, :, cols] = jnp.concatenate(halves, axis=0).T.astype(o_ref.dtype)


def _gqa_attention(proj3):
    nb, seq, _ = proj3.shape
    tq, tk = GQA_Q_TILE, GQA_K_TILE
    assert seq % tk == 0 and seq % tq == 0
    q_blk = 3 * NA_WIDTH // GQA_Q_WIDTH
    k_blk = (3 * NA_WIDTH + GQA_Q_WIDTH) // GQA_KV_WIDTH
    return pl.pallas_call(
        functools.partial(_gqa_kernel, seq=seq),
        grid=(nb, seq // tq),
        in_specs=[
            pl.BlockSpec((1, tq, GQA_Q_WIDTH), lambda b, i: (b, i, q_blk)),
            pl.BlockSpec((1, seq, GQA_KV_WIDTH), lambda b, i: (b, 0, k_blk)),
            pl.BlockSpec((1, seq, GQA_KV_WIDTH), lambda b, i: (b, 0, k_blk + 1)),
        ],
        out_specs=pl.BlockSpec((1, tq, GQA_Q_WIDTH), lambda b, i: (b, i, 0)),
        out_shape=jax.ShapeDtypeStruct((nb, seq, GQA_Q_WIDTH), jnp.bfloat16),
        scratch_shapes=[
            pltpu.VMEM((GQA_KV_HEADS, seq // tk, LANES, tk), jnp.bfloat16),
            pltpu.VMEM((seq, tq), jnp.float32),
        ],
        compiler_params=pltpu.CompilerParams(dimension_semantics=("parallel", "arbitrary"),
                                             vmem_limit_bytes=VMEM_LIMIT),
        name="gqa_attention",
    )(proj3, proj3, proj3)


def _sigmoid(x):
    return 1.0 / (1.0 + jnp.exp(-x))


def _out_proj_kernel(x_ref, oa_ref, ob_ref, g_ref, scale_ref, shift_ref, gate_ref, wzg_ref, bg_ref,
                     wpa_ref, wpb_ref, wout_ref, fg_ref, o_ref, *, final):
    x = x_ref[...]
    hid = _modulated_norm(x, g_ref[...], scale_ref[0], shift_ref[0])
    zg = jnp.dot(hid.astype(jnp.bfloat16), wzg_ref[...], preferred_element_type=jnp.float32)
    za = zg[:, :NA_WIDTH]
    zb = zg[:, NA_WIDTH:NA_WIDTH + GQA_Q_WIDTH]
    ua = oa_ref[...].astype(jnp.float32) * (za * _sigmoid(za))
    ub = ob_ref[...].astype(jnp.float32) * (zb * _sigmoid(zb))
    ya = jnp.dot(ua.astype(jnp.bfloat16), wpa_ref[...], preferred_element_type=jnp.float32)
    yb = jnp.dot(ub.astype(jnp.bfloat16), wpb_ref[...], preferred_element_type=jnp.float32)
    gates = _sigmoid(zg[:, NA_WIDTH + GQA_Q_WIDTH:] + bg_ref[...])
    merged = gates[:, :D_MODEL] * ya + gates[:, D_MODEL:] * yb
    y = jnp.dot(merged.astype(jnp.bfloat16), wout_ref[...], preferred_element_type=jnp.float32)
    out = x + gate_ref[0] * y
    if final:
        ms = jnp.mean(out * out, axis=-1, keepdims=True)
        out = out * lax.rsqrt(ms + EPS) * fg_ref[...]
    o_ref[...] = out


def _out_projection(x2, oa, ob, norm_g, scale, shift, gate, wzg, b_gate, wpa, wpb, wout, final_g, seq, final):
    tokens = x2.shape[0]
    tm = TOKEN_TILE
    per_seq = seq // tm
    vec = lambda i: (0, 0)
    per_batch = lambda i: (i // per_seq, 0, 0)
    return pl.pallas_call(
        functools.partial(_out_proj_kernel, final=final),
        grid=(tokens // tm,),
        in_specs=[
            pl.BlockSpec((tm, D_MODEL), lambda i: (i, 0)),
            pl.BlockSpec((tm, NA_WIDTH), lambda i: (i, 0)),
            pl.BlockSpec((tm, GQA_Q_WIDTH), lambda i: (i, 0)),
            pl.BlockSpec((1, D_MODEL), vec),
            pl.BlockSpec((1, 1, D_MODEL), per_batch),
            pl.BlockSpec((1, 1, D_MODEL), per_batch),
            pl.BlockSpec((1, 1, D_MODEL), per_batch),
            pl.BlockSpec((D_MODEL, ZG_WIDTH), vec),
            pl.BlockSpec((1, 2 * D_MODEL), vec),
            pl.BlockSpec((NA_WIDTH, D_MODEL), vec),
            pl.BlockSpec((GQA_Q_WIDTH, D_MODEL), vec),
            pl.BlockSpec((D_MODEL, D_MODEL), vec),
            pl.BlockSpec((1, D_MODEL), vec),
        ],
        out_specs=pl.BlockSpec((tm, D_MODEL), lambda i: (i, 0)),
        out_shape=jax.ShapeDtypeStruct((tokens, D_MODEL), jnp.float32),
        compiler_params=pltpu.CompilerParams(dimension_semantics=("parallel",), vmem_limit_bytes=VMEM_LIMIT),
        name="out_projection",
    )(x2, oa, ob, norm_g, scale, shift, gate, wzg, b_gate, wpa, wpb, wout, final_g)


def _rope_tables(seq):
    t = jnp.arange(seq)
    row = (t // GRID_W).astype(jnp.float32)
    col = (t % GRID_W).astype(jnp.float32)
    inv = ROPE_THETA ** (-jnp.arange(0, ROPE_AXIS_DIM, 2, dtype=jnp.float32) / ROPE_AXIS_DIM)
    ang_r, ang_c = row[:, None] * inv, col[:, None] * inv
    cos = jnp.concatenate([jnp.cos(ang_r)] * 2 + [jnp.cos(ang_c)] * 2, axis=-1)
    sin = jnp.concatenate([-jnp.sin(ang_r), jnp.sin(ang_r), -jnp.sin(ang_c), jnp.sin(ang_c)], axis=-1)
    return jnp.tile(cos, (1, 2)), jnp.tile(sin, (1, 2))


def _prepare_layer(l, norm_g, w_in, b_gate, rpb, q_norm_g, k_norm_g, w_pa, w_pb, w_out):
    bounds = np.concatenate([[0], np.cumsum(IN_SPLITS)])
    qa, ka, va, za, qb, kb, vb, zb, ga, gb = (w_in[l][:, bounds[i]:bounds[i + 1]] for i in range(10))
    order = np.concatenate([np.arange(HEAD_DIM) + HEAD_DIM * h for h in _GQA_HEAD_ORDER])
    bf = jnp.bfloat16
    return dict(
        norm_g=norm_g[l][None, :],
        w1=jnp.concatenate([qa * ATTN_SCALE, ka, va, qb[:, order], kb, vb], axis=1).astype(bf),
        wzg=jnp.concatenate([za, zb[:, order], ga, gb], axis=1).astype(bf),
        b_gate=b_gate[l][None, :],
        bias=_na_bias_tables(rpb[l]),
        qg=jnp.tile(q_norm_g[l] * ATTN_SCALE, 2)[None, :],
        kg=jnp.tile(k_norm_g[l], 2)[None, :],
        wpa=w_pa[l].astype(bf),
        wpb=w_pb[l][order, :].astype(bf),
        wout=w_out[l].astype(bf),
    )


def _trunk(x, mod, layers, final_g):
    nb, seq, _ = x.shape
    x2 = x.reshape(nb * seq, D_MODEL)
    cos_t, sin_t = _rope_tables(seq)
    for l, p in enumerate(layers):
        shift, scale, gate = (mod[l][:, None, i * D_MODEL:(i + 1) * D_MODEL] for i in range(3))
        proj = _in_projection(x2, p["norm_g"], scale, shift, p["w1"], cos_t, sin_t, p["qg"], p["kg"], seq)
        proj3 = proj.reshape(nb, seq, PROJ_WIDTH)
        oa = _na_attention(proj3, p["bias"]).reshape(nb * seq, NA_WIDTH)
        ob = _gqa_attention(proj3).reshape(nb * seq, GQA_Q_WIDTH)
        x2 = _out_projection(x2, oa, ob, p["norm_g"], scale, shift, gate, p["wzg"], p["b_gate"],
                             p["wpa"], p["wpb"], p["wout"], final_g[None, :], seq, final=(l == len(layers) - 1))
    return x2.reshape(nb, seq, D_MODEL)


def kernel(x_prompt, x_sample, c_prompt, c_sample, norm_g, w_ada, b_ada, w_in, b_gate, rpb, q_norm_g, k_norm_g, w_pa, w_pb, w_out, final_g):
    layers = [_prepare_layer(l, norm_g, w_in, b_gate, rpb, q_norm_g, k_norm_g, w_pa, w_pb, w_out)
              for l in range(DEPTH)]
    n_prompt = c_prompt.shape[0]
    mod = _modulation(jnp.concatenate([c_prompt, c_sample], axis=0), w_ada, b_ada)
    y_prompt = _trunk(x_prompt, mod[:, :n_prompt], layers, final_g)
    y_sample = _trunk(x_sample, mod[:, n_prompt:], layers, final_g)
    return (y_prompt, y_sample)
```

```python
import functools

import jax
import jax.numpy as jnp
import numpy as np
from jax import lax
from jax.experimental import pallas as pl
from jax.experimental.pallas import tpu as pltpu

D_MODEL = 1024
DEPTH = 4
GRID_W = 64
HEAD_DIM = 64
NA_HEADS = 8
NA_WIDTH = NA_HEADS * HEAD_DIM
NA_WIN_H = 8
NA_WIN_W = 16
GQA_Q_HEADS = 8
GQA_KV_HEADS = 2
GQA_GROUP = GQA_Q_HEADS // GQA_KV_HEADS
GQA_Q_WIDTH = GQA_Q_HEADS * HEAD_DIM
GQA_KV_WIDTH = GQA_KV_HEADS * HEAD_DIM
ROPE_AXIS_DIM = HEAD_DIM // 2
ROPE_THETA = 10000.0
EPS = 1e-6
ATTN_SCALE = HEAD_DIM ** -0.5
IN_SPLITS = (NA_WIDTH,) * 4 + (GQA_Q_WIDTH, GQA_KV_WIDTH, GQA_KV_WIDTH, GQA_Q_WIDTH, D_MODEL, D_MODEL)
MASKED = -1e30

LANES = 128
PROJ_WIDTH = 3 * NA_WIDTH + GQA_Q_WIDTH + 2 * GQA_KV_WIDTH
ZG_WIDTH = NA_WIDTH + GQA_Q_WIDTH + 2 * D_MODEL
TOKEN_TILE = 512
NA_ROWS = 4
NA_BAND = 12
GQA_Q_TILE = 256
GQA_K_TILE = 512
GQA_UNROLL = 4
VMEM_LIMIT = 56 * 1024 * 1024

_GQA_HEAD_ORDER = (0, 4, 1, 5, 2, 6, 3, 7)


def _low_half(shape):
    return lax.broadcasted_iota(jnp.int32, shape, len(shape) - 1) < HEAD_DIM


def _mod_kernel(c_ref, w_ref, b_ref, o_ref):
    c = c_ref[...]
    sc = c * (1.0 / (1.0 + jnp.exp(-c)))
    o_ref[0] = jnp.dot(sc, w_ref[0], precision=lax.Precision.HIGHEST,
                       preferred_element_type=jnp.float32) + b_ref[0]


def _modulation(c_all, w_ada, b_ada):
    nb = c_all.shape[0]
    n_chunks = 3
    return pl.pallas_call(
        _mod_kernel,
        grid=(DEPTH, n_chunks),
        in_specs=[
            pl.BlockSpec((nb, D_MODEL), lambda l, j: (0, 0)),
            pl.BlockSpec((1, D_MODEL, D_MODEL), lambda l, j: (l, 0, j)),
            pl.BlockSpec((1, 1, D_MODEL), lambda l, j: (l, 0, j)),
        ],
        out_specs=pl.BlockSpec((1, nb, D_MODEL), lambda l, j: (l, 0, j)),
        out_shape=jax.ShapeDtypeStruct((DEPTH, nb, 3 * D_MODEL), jnp.float32),
        compiler_params=pltpu.CompilerParams(vmem_limit_bytes=VMEM_LIMIT),
        name="adaln_modulation",
    )(c_all, w_ada, b_ada.reshape(DEPTH, 1, 3 * D_MODEL))


def _modulated_norm(x, g, scale, shift):
    ms = jnp.mean(x * x, axis=-1, keepdims=True)
    y = x * lax.rsqrt(ms + EPS) * g
    return y * (1.0 + scale) + shift


def _head_norm_rope(x, g, cos, sin_signed, low):
    sq = x * x
    lo = jnp.sum(jnp.where(low, sq, 0.0), axis=-1, keepdims=True)
    hi = jnp.sum(jnp.where(low, 0.0, sq), axis=-1, keepdims=True)
    ms = jnp.where(low, lo, hi) * (1.0 / HEAD_DIM)
    y = x * lax.rsqrt(ms + EPS) * g
    lane = lax.broadcasted_iota(jnp.int32, x.shape, 1)
    quarter = ROPE_AXIS_DIM // 2
    partner = jnp.where((lane & quarter) != 0, pltpu.roll(y, quarter, 1), pltpu.roll(y, LANES - quarter, 1))
    return y * cos + partner * sin_signed


def _in_proj_kernel(x_ref, g_ref, scale_ref, shift_ref, w_ref, cos_ref, sin_ref, qg_ref, kg_ref, o_ref):
    hid = _modulated_norm(x_ref[...], g_ref[...], scale_ref[0], shift_ref[0])
    acc = jnp.dot(hid.astype(jnp.bfloat16), w_ref[...], preferred_element_type=jnp.float32)
    na_cols = 3 * NA_WIDTH
    o_ref[:, :na_cols] = acc[:, :na_cols].astype(o_ref.dtype)
    cos = cos_ref[...]
    sin = sin_ref[...]
    low = _low_half(cos.shape)
    n_q_chunks = GQA_Q_WIDTH // LANES
    for j in range(n_q_chunks + 1):
        c0 = na_cols + j * LANES
        gain = qg_ref[...] if j < n_q_chunks else kg_ref[...]
        o_ref[:, c0:c0 + LANES] = _head_norm_rope(acc[:, c0:c0 + LANES], gain, cos, sin, low).astype(o_ref.dtype)
    v0 = na_cols + GQA_Q_WIDTH + GQA_KV_WIDTH
    o_ref[:, v0:] = acc[:, v0:].astype(o_ref.dtype)


def _in_projection(x2, norm_g, scale, shift, w1, cos_t, sin_t, qg, kg, seq):
    tokens = x2.shape[0]
    tm = TOKEN_TILE
    per_seq = seq // tm
    vec = lambda i: (0, 0)
    return pl.pallas_call(
        _in_proj_kernel,
        grid=(tokens // tm,),
        in_specs=[
            pl.BlockSpec((tm, D_MODEL), lambda i: (i, 0)),
            pl.BlockSpec((1, D_MODEL), vec),
            pl.BlockSpec((1, 1, D_MODEL), lambda i: (i // per_seq, 0, 0)),
            pl.BlockSpec((1, 1, D_MODEL), lambda i: (i // per_seq, 0, 0)),
            pl.BlockSpec((D_MODEL, PROJ_WIDTH), vec),
            pl.BlockSpec((tm, LANES), lambda i: (i % per_seq, 0)),
            pl.BlockSpec((tm, LANES), lambda i: (i % per_seq, 0)),
            pl.BlockSpec((1, LANES), vec),
            pl.BlockSpec((1, LANES), vec),
        ],
        out_specs=pl.BlockSpec((tm, PROJ_WIDTH), lambda i: (i, 0)),
        out_shape=jax.ShapeDtypeStruct((tokens, PROJ_WIDTH), jnp.bfloat16),
        compiler_params=pltpu.CompilerParams(dimension_semantics=("parallel",), vmem_limit_bytes=VMEM_LIMIT),
        name="in_projection",
    )(x2, norm_g, scale, shift, w1, cos_t, sin_t, qg, kg)


def _na_kernel(q_ref, k_ref, v_ref, bias_ref, o_ref, *, grid_rows):
    g = pl.program_id(1)
    band_row = jnp.clip(NA_ROWS * g - NA_WIN_H // 2, 0, grid_rows - NA_BAND)
    k0 = pl.multiple_of(band_row * GRID_W, NA_ROWS * GRID_W)
    nk = NA_BAND * GRID_W
    nq = NA_ROWS * GRID_W
    low = _low_half((nq, LANES))
    for c in range(NA_WIDTH // LANES):
        cols = slice(c * LANES, (c + 1) * LANES)
        kc = k_ref[0, pl.ds(k0, nk), cols]
        vc = v_ref[0, pl.ds(k0, nk), cols]
        qc = q_ref[0, :, cols].astype(jnp.float32)
        out = jnp.zeros((nq, LANES), jnp.float32)
        for half in range(2):
            own = low if half == 0 else jnp.logical_not(low)
            qm = jnp.where(own, qc, 0.0).astype(jnp.bfloat16)
            s = lax.dot_general(qm, kc, (((1,), (1,)), ((), ())), preferred_element_type=jnp.float32)
            s = s + bias_ref[0, 2 * c + half]
            m = jnp.max(s, axis=-1, keepdims=True)
            p = jnp.exp(s - m)
            l = jnp.sum(p, axis=-1, keepdims=True)
            o = jnp.dot(p.astype(jnp.bfloat16), vc, preferred_element_type=jnp.float32) / l
            out = jnp.where(own, o, out)
        o_ref[0, :, cols] = out.astype(o_ref.dtype)


def _na_attention(proj3, bias):
    nb, seq, _ = proj3.shape
    grid_rows = seq // GRID_W
    assert grid_rows % NA_ROWS == 0 and grid_rows >= NA_BAND
    steps = grid_rows // NA_ROWS
    nq = NA_ROWS * GRID_W
    variant = lambda b, g: ((g > 0).astype(jnp.int32) + (g == steps - 1).astype(jnp.int32), 0, 0, 0)
    return pl.pallas_call(
        functools.partial(_na_kernel, grid_rows=grid_rows),
        grid=(nb, steps),
        in_specs=[
            pl.BlockSpec((1, nq, NA_WIDTH), lambda b, g: (b, g, 0)),
            pl.BlockSpec((1, seq, NA_WIDTH), lambda b, g: (b, 0, 1)),
            pl.BlockSpec((1, seq, NA_WIDTH), lambda b, g: (b, 0, 2)),
            pl.BlockSpec((1, NA_HEADS, nq, NA_BAND * GRID_W), variant),
        ],
        out_specs=pl.BlockSpec((1, nq, NA_WIDTH), lambda b, g: (b, g, 0)),
        out_shape=jax.ShapeDtypeStruct((nb, seq, NA_WIDTH), jnp.bfloat16),
        compiler_params=pltpu.CompilerParams(dimension_semantics=("parallel", "arbitrary"),
                                             vmem_limit_bytes=VMEM_LIMIT),
        name="neighbourhood_attention",
    )(proj3, proj3, proj3, bias)


def _na_bias_tables(rpb_l):
    i = np.arange(NA_ROWS)[:, None]
    a = np.arange(NA_BAND)[None, :]
    d = a - i
    first = np.stack([np.zeros_like(d) - i, np.zeros_like(d), np.zeros_like(d) + 4 - i])
    start = np.stack([np.zeros_like(d), np.zeros_like(d) + i, np.zeros_like(d) + 4])
    del first
    row_ok = (a[None] >= start) & (a[None] < start + NA_WIN_H)
    row_off = np.stack([d + 7, d + 3, d - 1])
    row_off = np.clip(row_off, 0, 2 * NA_WIN_H - 2)
    cq = np.arange(GRID_W)[:, None]
    ck = np.arange(GRID_W)[None, :]
    c0 = np.clip(cq - NA_WIN_W // 2, 0, GRID_W - NA_WIN_W)
    col_ok = (ck >= c0) & (ck < c0 + NA_WIN_W)
    col_off = np.clip(ck - cq + NA_WIN_W - 1, 0, 2 * NA_WIN_W - 2)
    t = rpb_l[:, row_off][:, :, :, :, col_off]
    ok = row_ok[:, :, :, None, None] & col_ok[None, None, None]
    t = jnp.where(ok[None], t.astype(jnp.float32), MASKED)
    t = jnp.transpose(t, (1, 0, 2, 4, 3, 5))
    return t.reshape(3, NA_HEADS, NA_ROWS * GRID_W, NA_BAND * GRID_W)


def _gqa_kernel(q_ref, k_ref, v_ref, o_ref, vt_ref, s0_ref, s1_ref, *, seq):
    tq, tk = GQA_Q_TILE, GQA_K_TILE
    n_chunks = seq // tk
    s_refs = (s0_ref, s1_ref)
    MAX_ROWS = 32

    @pl.when(pl.program_id(1) == 0)
    def _():
        ones = jnp.ones((HEAD_DIM, tk), jnp.bfloat16)
        for j in range(n_chunks):
            t = v_ref[0, j * tk:(j + 1) * tk, :].astype(jnp.float32).T
            for hk in range(GQA_KV_HEADS):
                vt_ref[hk, j, :HEAD_DIM, :] = t[hk * HEAD_DIM:(hk + 1) * HEAD_DIM].astype(jnp.bfloat16)
                vt_ref[hk, j, HEAD_DIM:, :] = ones

    low = _low_half((tq, LANES))
    heads = [(c, hk) for c in range(GQA_Q_WIDTH // LANES) for hk in range(GQA_KV_HEADS)]

    def masked_q(c, hk):
        qc = q_ref[0, :, c * LANES:(c + 1) * LANES].astype(jnp.float32)
        own = low if hk == 0 else jnp.logical_not(low)
        return jnp.where(own, qc, 0.0).astype(jnp.bfloat16)

    def scores(j, qm, slot, mx):
        r0 = pl.multiple_of(j * tk, tk)
        s = lax.dot_general(k_ref[0, pl.ds(r0, tk), :], qm, (((1,), (1,)), ((), ())),
                            preferred_element_type=jnp.float32)
        s_refs[slot][pl.ds(r0, tk), :] = s
        return jnp.maximum(mx, jnp.max(s.reshape(tk // MAX_ROWS, MAX_ROWS, tq), axis=0))

    def weighted(j, hk, slot, m, acc):
        r0 = pl.multiple_of(j * tk, tk)
        p = jnp.exp(s_refs[slot][pl.ds(r0, tk), :] - m).astype(jnp.bfloat16)
        return acc + jnp.dot(vt_ref[hk, j], p, preferred_element_type=jnp.float32)

    mx0 = jnp.full((MAX_ROWS, tq), -jnp.inf, jnp.float32)
    acc0 = jnp.zeros((LANES, tq), jnp.float32)
    qm = masked_q(*heads[0])
    mx = lax.fori_loop(0, n_chunks, lambda j, mx, qm=qm: scores(j, qm, 0, mx), mx0, unroll=GQA_UNROLL)
    halves = []
    for i in range(1, len(heads) + 1):
        m = jnp.max(mx, axis=0, keepdims=True)
        c_prev, hk_prev = heads[i - 1]
        cur = (i - 1) % 2
        if i < len(heads):
            qm = masked_q(*heads[i])

            def both(j, carry, qm=qm, m=m, hk_prev=hk_prev, cur=cur):
                mx, acc = carry
                return scores(j, qm, 1 - cur, mx), weighted(j, hk_prev, cur, m, acc)

            mx, acc = lax.fori_loop(0, n_chunks, both, (mx0, acc0), unroll=GQA_UNROLL)
        else:
            acc = lax.fori_loop(0, n_chunks,
                                lambda j, acc, m=m, hk_prev=hk_prev, cur=cur: weighted(j, hk_prev, cur, m, acc),
                                acc0, unroll=GQA_UNROLL)
        halves.append(acc[:HEAD_DIM] / acc[HEAD_DIM:HEAD_DIM + 1])
        if hk_prev == GQA_KV_HEADS - 1:
            o_ref[0, :, c_prev * LANES:(c_prev + 1) * LANES] = jnp.concatenate(halves, axis=0).T.astype(o_ref.dtype)
            halves = []


def _gqa_attention(proj3):
    nb, seq, _ = proj3.shape
    tq, tk = GQA_Q_TILE, GQA_K_TILE
    assert seq % tk == 0 and seq % tq == 0
    q_blk = 3 * NA_WIDTH // GQA_Q_WIDTH
    k_blk = (3 * NA_WIDTH + GQA_Q_WIDTH) // GQA_KV_WIDTH
    return pl.pallas_call(
        functools.partial(_gqa_kernel, seq=seq),
        grid=(nb, seq // tq),
        in_specs=[
            pl.BlockSpec((1, tq, GQA_Q_WIDTH), lambda b, i: (b, i, q_blk)),
            pl.BlockSpec((1, seq, GQA_KV_WIDTH), lambda b, i: (b, 0, k_blk)),
            pl.BlockSpec((1, seq, GQA_KV_WIDTH), lambda b, i: (b, 0, k_blk + 1)),
        ],
        out_specs=pl.BlockSpec((1, tq, GQA_Q_WIDTH), lambda b, i: (b, i, 0)),
        out_shape=jax.ShapeDtypeStruct((nb, seq, GQA_Q_WIDTH), jnp.bfloat16),
        scratch_shapes=[
            pltpu.VMEM((GQA_KV_HEADS, seq // tk, LANES, tk), jnp.bfloat16),
            pltpu.VMEM((seq, tq), jnp.float32),
            pltpu.VMEM((seq, tq), jnp.float32),
        ],
        compiler_params=pltpu.CompilerParams(dimension_semantics=("parallel", "arbitrary"),
                                             vmem_limit_bytes=VMEM_LIMIT),
        name="gqa_attention",
    )(proj3, proj3, proj3)


def _sigmoid(x):
    return 1.0 / (1.0 + jnp.exp(-x))


def _out_proj_kernel(x_ref, oa_ref, ob_ref, g_ref, scale_ref, shift_ref, gate_ref, wzg_ref, bg_ref,
                     wpa_ref, wpb_ref, wout_ref, fg_ref, o_ref, *, final):
    x = x_ref[...]
    hid = _modulated_norm(x, g_ref[...], scale_ref[0], shift_ref[0])
    zg = jnp.dot(hid.astype(jnp.bfloat16), wzg_ref[...], preferred_element_type=jnp.float32)
    za = zg[:, :NA_WIDTH]
    zb = zg[:, NA_WIDTH:NA_WIDTH + GQA_Q_WIDTH]
    ua = oa_ref[...].astype(jnp.float32) * (za * _sigmoid(za))
    ub = ob_ref[...].astype(jnp.float32) * (zb * _sigmoid(zb))
    ya = jnp.dot(ua.astype(jnp.bfloat16), wpa_ref[...], preferred_element_type=jnp.float32)
    yb = jnp.dot(ub.astype(jnp.bfloat16), wpb_ref[...], preferred_element_type=jnp.float32)
    gates = _sigmoid(zg[:, NA_WIDTH + GQA_Q_WIDTH:] + bg_ref[...])
    merged = gates[:, :D_MODEL] * ya + gates[:, D_MODEL:] * yb
    y = jnp.dot(merged.astype(jnp.bfloat16), wout_ref[...], preferred_element_type=jnp.float32)
    out = x + gate_ref[0] * y
    if final:
        ms = jnp.mean(out * out, axis=-1, keepdims=True)
        out = out * lax.rsqrt(ms + EPS) * fg_ref[...]
    o_ref[...] = out


def _out_projection(x2, oa, ob, norm_g, scale, shift, gate, wzg, b_gate, wpa, wpb, wout, final_g, seq, final):
    tokens = x2.shape[0]
    tm = TOKEN_TILE
    per_seq = seq // tm
    vec = lambda i: (0, 0)
    per_batch = lambda i: (i // per_seq, 0, 0)
    return pl.pallas_call(
        functools.partial(_out_proj_kernel, final=final),
        grid=(tokens // tm,),
        in_specs=[
            pl.BlockSpec((tm, D_MODEL), lambda i: (i, 0)),
            pl.BlockSpec((tm, NA_WIDTH), lambda i: (i, 0)),
            pl.BlockSpec((tm, GQA_Q_WIDTH), lambda i: (i, 0)),
            pl.BlockSpec((1, D_MODEL), vec),
            pl.BlockSpec((1, 1, D_MODEL), per_batch),
            pl.BlockSpec((1, 1, D_MODEL), per_batch),
            pl.BlockSpec((1, 1, D_MODEL), per_batch),
            pl.BlockSpec((D_MODEL, ZG_WIDTH), vec),
            pl.BlockSpec((1, 2 * D_MODEL), vec),
            pl.BlockSpec((NA_WIDTH, D_MODEL), vec),
            pl.BlockSpec((GQA_Q_WIDTH, D_MODEL), vec),
            pl.BlockSpec((D_MODEL, D_MODEL), vec),
            pl.BlockSpec((1, D_MODEL), vec),
        ],
        out_specs=pl.BlockSpec((tm, D_MODEL), lambda i: (i, 0)),
        out_shape=jax.ShapeDtypeStruct((tokens, D_MODEL), jnp.float32),
        compiler_params=pltpu.CompilerParams(dimension_semantics=("parallel",), vmem_limit_bytes=VMEM_LIMIT),
        name="out_projection",
    )(x2, oa, ob, norm_g, scale, shift, gate, wzg, b_gate, wpa, wpb, wout, final_g)


def _rope_tables(seq):
    t = jnp.arange(seq)
    row = (t // GRID_W).astype(jnp.float32)
    col = (t % GRID_W).astype(jnp.float32)
    inv = ROPE_THETA ** (-jnp.arange(0, ROPE_AXIS_DIM, 2, dtype=jnp.float32) / ROPE_AXIS_DIM)
    ang_r, ang_c = row[:, None] * inv, col[:, None] * inv
    cos = jnp.concatenate([jnp.cos(ang_r)] * 2 + [jnp.cos(ang_c)] * 2, axis=-1)
    sin = jnp.concatenate([-jnp.sin(ang_r), jnp.sin(ang_r), -jnp.sin(ang_c), jnp.sin(ang_c)], axis=-1)
    return jnp.tile(cos, (1, 2)), jnp.tile(sin, (1, 2))


def _prepare_layer(l, norm_g, w_in, b_gate, rpb, q_norm_g, k_norm_g, w_pa, w_pb, w_out):
    bounds = np.concatenate([[0], np.cumsum(IN_SPLITS)])
    qa, ka, va, za, qb, kb, vb, zb, ga, gb = (w_in[l][:, bounds[i]:bounds[i + 1]] for i in range(10))
    order = np.concatenate([np.arange(HEAD_DIM) + HEAD_DIM * h for h in _GQA_HEAD_ORDER])
    bf = jnp.bfloat16
    return dict(
        norm_g=norm_g[l][None, :],
        w1=jnp.concatenate([qa * ATTN_SCALE, ka, va, qb[:, order], kb, vb], axis=1).astype(bf),
        wzg=jnp.concatenate([za, zb[:, order], ga, gb], axis=1).astype(bf),
        b_gate=b_gate[l][None, :],
        bias=_na_bias_tables(rpb[l]),
        qg=jnp.tile(q_norm_g[l] * ATTN_SCALE, 2)[None, :],
        kg=jnp.tile(k_norm_g[l], 2)[None, :],
        wpa=w_pa[l].astype(bf),
        wpb=w_pb[l][order, :].astype(bf),
        wout=w_out[l].astype(bf),
    )


def _trunk(x, mod, layers, final_g):
    nb, seq, _ = x.shape
    x2 = x.reshape(nb * seq, D_MODEL)
    cos_t, sin_t = _rope_tables(seq)
    for l, p in enumerate(layers):
        shift, scale, gate = (mod[l][:, None, i * D_MODEL:(i + 1) * D_MODEL] for i in range(3))
        proj = _in_projection(x2, p["norm_g"], scale, shift, p["w1"], cos_t, sin_t, p["qg"], p["kg"], seq)
        proj3 = proj.reshape(nb, seq, PROJ_WIDTH)
        oa = _na_attention(proj3, p["bias"]).reshape(nb * seq, NA_WIDTH)
        ob = _gqa_attention(proj3).reshape(nb * seq, GQA_Q_WIDTH)
        x2 = _out_projection(x2, oa, ob, p["norm_g"], scale, shift, gate, p["wzg"], p["b_gate"],
                             p["wpa"], p["wpb"], p["wout"], final_g[None, :], seq, final=(l == len(layers) - 1))
    return x2.reshape(nb, seq, D_MODEL)


def kernel(x_prompt, x_sample, c_prompt, c_sample, norm_g, w_ada, b_ada, w_in, b_gate, rpb, q_norm_g, k_norm_g, w_pa, w_pb, w_out, final_g):
    layers = [_prepare_layer(l, norm_g, w_in, b_gate, rpb, q_norm_g, k_norm_g, w_pa, w_pb, w_out)
              for l in range(DEPTH)]
    n_prompt = c_prompt.shape[0]
    mod = _modulation(jnp.concatenate([c_prompt, c_sample], axis=0), w_ada, b_ada)
    y_prompt = _trunk(x_prompt, mod[:, :n_prompt], layers, final_g)
    y_sample = _trunk(x_sample, mod[:, n_prompt:], layers, final_g)
    return (y_prompt, y_sample)
```

```python
import functools

import jax
import jax.numpy as jnp
import numpy as np
from jax import lax
from jax.experimental import pallas as pl
from jax.experimental.pallas import tpu as pltpu

D_MODEL = 1024
DEPTH = 4
GRID_W = 64
HEAD_DIM = 64
NA_HEADS = 8
NA_WIDTH = NA_HEADS * HEAD_DIM
NA_WIN_H = 8
NA_WIN_W = 16
GQA_Q_HEADS = 8
GQA_KV_HEADS = 2
GQA_GROUP = GQA_Q_HEADS // GQA_KV_HEADS
GQA_Q_WIDTH = GQA_Q_HEADS * HEAD_DIM
GQA_KV_WIDTH = GQA_KV_HEADS * HEAD_DIM
ROPE_AXIS_DIM = HEAD_DIM // 2
ROPE_THETA = 10000.0
EPS = 1e-6
ATTN_SCALE = HEAD_DIM ** -0.5
IN_SPLITS = (NA_WIDTH,) * 4 + (GQA_Q_WIDTH, GQA_KV_WIDTH, GQA_KV_WIDTH, GQA_Q_WIDTH, D_MODEL, D_MODEL)
MASKED = -1e30
LOG2_E = 1.4426950408889634

LANES = 128
NA_PROJ_WIDTH = 3 * NA_WIDTH
GQA_PROJ_WIDTH = GQA_Q_WIDTH + 2 * GQA_KV_WIDTH
ZG_WIDTH = NA_WIDTH + GQA_Q_WIDTH + 2 * D_MODEL
TOKEN_TILE = 512
NA_ROWS = 4
NA_BAND = 12
NA_KEY_TILE = NA_ROWS * GRID_W
GQA_SCORE_BYTES = 16 * 1024 * 1024
GQA_K_TILE = 512
GQA_MAX_UNROLL = 8
VMEM_LIMIT = 56 * 1024 * 1024

_GQA_HEAD_ORDER = (0, 4, 1, 5, 2, 6, 3, 7)


def _low_half(shape):
    return lax.broadcasted_iota(jnp.int32, shape, len(shape) - 1) < HEAD_DIM


def _mod_kernel(c_ref, w_ref, b_ref, o_ref):
    c = c_ref[...]
    sc = c * (1.0 / (1.0 + jnp.exp(-c)))
    o_ref[0] = jnp.dot(sc, w_ref[0], precision=lax.Precision.HIGHEST,
                       preferred_element_type=jnp.float32) + b_ref[0]


def _modulation(c_all, w_ada, b_ada):
    nb = c_all.shape[0]
    n_chunks = 3
    return pl.pallas_call(
        _mod_kernel,
        grid=(DEPTH, n_chunks),
        in_specs=[
            pl.BlockSpec((nb, D_MODEL), lambda l, j: (0, 0)),
            pl.BlockSpec((1, D_MODEL, D_MODEL), lambda l, j: (l, 0, j)),
            pl.BlockSpec((1, 1, D_MODEL), lambda l, j: (l, 0, j)),
        ],
        out_specs=pl.BlockSpec((1, nb, D_MODEL), lambda l, j: (l, 0, j)),
        out_shape=jax.ShapeDtypeStruct((DEPTH, nb, 3 * D_MODEL), jnp.float32),
        compiler_params=pltpu.CompilerParams(vmem_limit_bytes=VMEM_LIMIT),
        name="adaln_modulation",
    )(c_all, w_ada, b_ada.reshape(DEPTH, 1, 3 * D_MODEL))


def _modulated_norm(x, g, scale, shift):
    ms = jnp.mean(x * x, axis=-1, keepdims=True)
    y = x * lax.rsqrt(ms + EPS) * g
    return y * (1.0 + scale) + shift


def _head_norm_rope(x, g, cos, sin_signed, low):
    sq = x * x
    lo = jnp.sum(jnp.where(low, sq, 0.0), axis=-1, keepdims=True)
    hi = jnp.sum(jnp.where(low, 0.0, sq), axis=-1, keepdims=True)
    ms = jnp.where(low, lo, hi) * (1.0 / HEAD_DIM)
    y = x * lax.rsqrt(ms + EPS) * g
    lane = lax.broadcasted_iota(jnp.int32, x.shape, 1)
    quarter = ROPE_AXIS_DIM // 2
    partner = jnp.where((lane & quarter) != 0, pltpu.roll(y, quarter, 1), pltpu.roll(y, LANES - quarter, 1))
    return y * cos + partner * sin_signed


def _in_proj_kernel(x_ref, g_ref, scale_ref, shift_ref, wb_ref, wa_ref, cos_ref, sin_ref, qg_ref, kg_ref,
                    ob_ref, oa_ref):
    hid = _modulated_norm(x_ref[...], g_ref[...], scale_ref[0], shift_ref[0]).astype(jnp.bfloat16)
    acc = jnp.dot(hid, wb_ref[...], preferred_element_type=jnp.float32)
    cos = cos_ref[...]
    sin = sin_ref[...]
    low = _low_half(cos.shape)
    n_q_chunks = GQA_Q_WIDTH // LANES
    for j in range(n_q_chunks + 1):
        c0 = j * LANES
        gain = qg_ref[...] if j < n_q_chunks else kg_ref[...]
        ob_ref[:, c0:c0 + LANES] = _head_norm_rope(acc[:, c0:c0 + LANES], gain, cos, sin, low).astype(ob_ref.dtype)
    v0 = GQA_Q_WIDTH + GQA_KV_WIDTH
    ob_ref[:, v0:] = acc[:, v0:].astype(ob_ref.dtype)
    oa_ref[...] = jnp.dot(hid, wa_ref[...], preferred_element_type=jnp.float32).astype(oa_ref.dtype)


def _in_projection(x2, norm_g, scale, shift, wb, wa, cos_t, sin_t, qg, kg, seq):
    tokens = x2.shape[0]
    tm = TOKEN_TILE
    per_seq = seq // tm
    vec = lambda i: (0, 0)
    return pl.pallas_call(
        _in_proj_kernel,
        grid=(tokens // tm,),
        in_specs=[
            pl.BlockSpec((tm, D_MODEL), lambda i: (i, 0)),
            pl.BlockSpec((1, D_MODEL), vec),
            pl.BlockSpec((1, 1, D_MODEL), lambda i: (i // per_seq, 0, 0)),
            pl.BlockSpec((1, 1, D_MODEL), lambda i: (i // per_seq, 0, 0)),
            pl.BlockSpec((D_MODEL, GQA_PROJ_WIDTH), vec),
            pl.BlockSpec((D_MODEL, NA_PROJ_WIDTH), vec),
            pl.BlockSpec((tm, LANES), lambda i: (i % per_seq, 0)),
            pl.BlockSpec((tm, LANES), lambda i: (i % per_seq, 0)),
            pl.BlockSpec((1, LANES), vec),
            pl.BlockSpec((1, LANES), vec),
        ],
        out_specs=[pl.BlockSpec((tm, GQA_PROJ_WIDTH), lambda i: (i, 0)),
                   pl.BlockSpec((tm, NA_PROJ_WIDTH), lambda i: (i, 0))],
        out_shape=[jax.ShapeDtypeStruct((tokens, GQA_PROJ_WIDTH), jnp.bfloat16),
                   jax.ShapeDtypeStruct((tokens, NA_PROJ_WIDTH), jnp.bfloat16)],
        compiler_params=pltpu.CompilerParams(dimension_semantics=("parallel",), vmem_limit_bytes=VMEM_LIMIT),
        name="in_projection",
    )(x2, norm_g, scale, shift, wb, wa, cos_t, sin_t, qg, kg)


def _na_kernel(q_ref, k_ref, v_ref, bias_ref, o_ref, vt_ref, s0_ref, s1_ref, *, grid_rows):
    g = pl.program_id(1)
    seq = grid_rows * GRID_W
    nk = NA_BAND * GRID_W
    nq = NA_ROWS * GRID_W
    kt = NA_KEY_TILE
    s_refs = (s0_ref, s1_ref)

    @pl.when(g == 0)
    def _():
        row = lax.broadcasted_iota(jnp.int32, (NA_WIDTH, kt), 0)
        even_rows = (row & HEAD_DIM) == 0
        for j in range(seq // kt):
            t = v_ref[0, j * kt:(j + 1) * kt, :].astype(jnp.float32).T
            vt_ref[0, j] = jnp.where(even_rows, t, 1.0).astype(jnp.bfloat16)
            vt_ref[1, j] = jnp.where(even_rows, 1.0, t).astype(jnp.bfloat16)

    band_row = jnp.clip(NA_ROWS * g - NA_WIN_H // 2, 0, grid_rows - NA_BAND)
    k0 = pl.multiple_of(band_row * GRID_W, kt)
    kb = band_row // (kt // GRID_W)
    low = _low_half((nq, LANES))

    def scores(c):
        qc = q_ref[0, :, c * LANES:(c + 1) * LANES].astype(jnp.float32)
        qm = jnp.concatenate([jnp.where(low, qc, 0.0), jnp.where(low, 0.0, qc)], axis=0).astype(jnp.bfloat16)
        s = lax.dot_general(k_ref[0, pl.ds(k0, nk), c * LANES:(c + 1) * LANES], qm, (((1,), (1,)), ((), ())),
                            preferred_element_type=jnp.float32)
        s = s + bias_ref[0, c]
        s_refs[c % 2][...] = s
        return jnp.max(jnp.max(s.reshape(nk // 8, 8, 2 * nq), axis=0), axis=0, keepdims=True)

    def weighted(c, m):
        p = jnp.exp2(s_refs[c % 2][...] - m).astype(jnp.bfloat16)
        halves = []
        for half in range(2):
            acc = jnp.zeros((LANES, nq), jnp.float32)
            for t in range(nk // kt):
                acc = acc + jnp.dot(vt_ref[half, kb + t, c * LANES:(c + 1) * LANES, :],
                                    p[t * kt:(t + 1) * kt, half * nq:(half + 1) * nq],
                                    preferred_element_type=jnp.float32)
            other = (1 - half) * HEAD_DIM
            halves.append(acc[half * HEAD_DIM:(half + 1) * HEAD_DIM] / acc[other:other + 1])
        o_ref[0, :, c * LANES:(c + 1) * LANES] = jnp.concatenate(halves, axis=0).T.astype(o_ref.dtype)

    n_pairs = NA_WIDTH // LANES
    m = scores(0)
    for c in range(1, n_pairs + 1):
        m_next = scores(c) if c < n_pairs else None
        weighted(c - 1, m)
        m = m_next


def _na_attention(proj3, bias):
    nb, seq, _ = proj3.shape
    grid_rows = seq // GRID_W
    assert grid_rows % NA_ROWS == 0 and grid_rows >= NA_BAND
    steps = grid_rows // NA_ROWS
    nq = NA_ROWS * GRID_W
    variant = lambda b, g: ((g > 0).astype(jnp.int32) + (g == steps - 1).astype(jnp.int32), 0, 0, 0)
    return pl.pallas_call(
        functools.partial(_na_kernel, grid_rows=grid_rows),
        grid=(nb, steps),
        in_specs=[
            pl.BlockSpec((1, nq, NA_WIDTH), lambda b, g: (b, g, 0)),
            pl.BlockSpec((1, seq, NA_WIDTH), lambda b, g: (b, 0, 1)),
            pl.BlockSpec((1, seq, NA_WIDTH), lambda b, g: (b, 0, 2)),
            pl.BlockSpec((1, NA_HEADS // 2, NA_BAND * GRID_W, 2 * nq), variant),
        ],
        out_specs=pl.BlockSpec((1, nq, NA_WIDTH), lambda b, g: (b, g, 0)),
        out_shape=jax.ShapeDtypeStruct((nb, seq, NA_WIDTH), jnp.bfloat16),
        scratch_shapes=[
            pltpu.VMEM((2, seq // NA_KEY_TILE, NA_WIDTH, NA_KEY_TILE), jnp.bfloat16),
            pltpu.VMEM((NA_BAND * GRID_W, 2 * nq), jnp.float32),
            pltpu.VMEM((NA_BAND * GRID_W, 2 * nq), jnp.float32),
        ],
        compiler_params=pltpu.CompilerParams(dimension_semantics=("parallel", "arbitrary"),
                                             vmem_limit_bytes=VMEM_LIMIT),
        name="neighbourhood_attention",
    )(proj3, proj3, proj3, bias)


def _na_bias_tables(rpb_l):
    i = np.arange(NA_ROWS)[:, None]
    a = np.arange(NA_BAND)[None, :]
    d = a - i
    start = np.stack([np.zeros_like(d), np.zeros_like(d) + i, np.zeros_like(d) + 4])
    row_ok = (a[None] >= start) & (a[None] < start + NA_WIN_H)
    row_off = np.stack([d + 7, d + 3, d - 1])
    row_off = np.clip(row_off, 0, 2 * NA_WIN_H - 2)
    cq = np.arange(GRID_W)[:, None]
    ck = np.arange(GRID_W)[None, :]
    c0 = np.clip(cq - NA_WIN_W // 2, 0, GRID_W - NA_WIN_W)
    col_ok = (ck >= c0) & (ck < c0 + NA_WIN_W)
    col_off = np.clip(ck - cq + NA_WIN_W - 1, 0, 2 * NA_WIN_W - 2)
    t = rpb_l[:, row_off][:, :, :, :, col_off]
    ok = row_ok[:, :, :, None, None] & col_ok[None, None, None]
    t = jnp.where(ok[None], t.astype(jnp.float32) * LOG2_E, MASKED)
    t = jnp.transpose(t, (1, 0, 3, 5, 2, 4))
    nk, nq = NA_BAND * GRID_W, NA_ROWS * GRID_W
    t = t.reshape(3, NA_HEADS // 2, 2, nk, nq)
    return jnp.transpose(t, (0, 1, 3, 2, 4)).reshape(3, NA_HEADS // 2, nk, 2 * nq)


def _gqa_kernel(trips_ref, q_ref, k_ref, v_ref, o_ref, vt_ref, s0_ref, s1_ref, acc_ref, *, seq):
    tq, tk = _gqa_q_tile(seq), GQA_K_TILE
    n_chunks = seq // tk
    unroll = _gqa_unroll(seq)
    s_refs = (s0_ref, s1_ref)
    MAX_ROWS = 8

    @pl.when(pl.program_id(1) == 0)
    def _():
        ones = jnp.ones((HEAD_DIM, tk), jnp.bfloat16)
        for j in range(n_chunks):
            t = v_ref[0, j * tk:(j + 1) * tk, :].astype(jnp.float32).T
            for hk in range(GQA_KV_HEADS):
                vt_ref[hk, j, :HEAD_DIM, :] = t[hk * HEAD_DIM:(hk + 1) * HEAD_DIM].astype(jnp.bfloat16)
                vt_ref[hk, j, HEAD_DIM:, :] = ones

    low = _low_half((tq, LANES))
    heads = [(c, hk) for c in range(GQA_Q_WIDTH // LANES) for hk in range(GQA_KV_HEADS)]

    def masked_q(c, hk):
        qc = q_ref[0, :, c * LANES:(c + 1) * LANES].astype(jnp.float32)
        own = low if hk == 0 else jnp.logical_not(low)
        return jnp.where(own, qc, 0.0).astype(jnp.bfloat16)

    def scores(j, qm, slot, mx):
        r0 = pl.multiple_of(j * tk, tk)
        s = lax.dot_general(k_ref[0, pl.ds(r0, tk), :], qm, (((1,), (1,)), ((), ())),
                            preferred_element_type=jnp.float32)
        s_refs[slot][pl.ds(r0, tk), :] = s
        return jnp.maximum(mx, jnp.max(s.reshape(tk // MAX_ROWS, MAX_ROWS, tq), axis=0))

    def weighted(j, hk, slot, m):
        r0 = pl.multiple_of(j * tk, tk)
        p = jnp.exp2(s_refs[slot][pl.ds(r0, tk), :] - m).astype(jnp.bfloat16)
        acc_ref[...] += jnp.dot(vt_ref[hk, j], p, preferred_element_type=jnp.float32)

    def chunk_loop(step, init):
        def body(t, carry):
            for u in range(unroll):
                carry = step(t * unroll + u, carry)
            return carry
        return lax.fori_loop(0, trips_ref[0], body, init)

    mx0 =jnp.full((MAX_ROWS, tq), -jnp.inf, jnp.float32)
    qm = masked_q(*heads[0])
    mx = chunk_loop(lambda j, mx, qm=qm: scores(j, qm, 0, mx), mx0)
    halves = []
    for i in range(1, len(heads) + 1):
        m = jnp.max(mx, axis=0, keepdims=True)
        c_prev, hk_prev = heads[i - 1]
        cur = (i - 1) % 2
        acc_ref[...] = jnp.zeros_like(acc_ref)
        if i < len(heads):
            qm = masked_q(*heads[i])

            def both(j, mx, qm=qm, m=m, hk_prev=hk_prev, cur=cur):
                weighted(j, hk_prev, cur, m)
                return scores(j, qm, 1 - cur, mx)

            mx = chunk_loop(both, mx0)
        else:
            def last(j, carry, m=m, hk_prev=hk_prev, cur=cur):
                weighted(j, hk_prev, cur, m)
                return carry

            chunk_loop(last, 0)
        acc = acc_ref[...]
        halves.append(acc[:HEAD_DIM] / acc[HEAD_DIM:HEAD_DIM + 1])
        if hk_prev == GQA_KV_HEADS - 1:
            o_ref[0, :, c_prev * LANES:(c_prev + 1) * LANES] = jnp.concatenate(halves, axis=0).T.astype(o_ref.dtype)
            halves = []


def _gqa_unroll(seq):
    return min(GQA_MAX_UNROLL, seq // GQA_K_TILE)


def _gqa_q_tile(seq):
    return min(seq, GQA_SCORE_BYTES // (2 * 4 * seq))


def _gqa_attention(proj3):
    nb, seq, _ = proj3.shape
    tq, tk = _gqa_q_tile(seq), GQA_K_TILE
    unroll = _gqa_unroll(seq)
    assert seq % (tk * unroll) == 0 and seq % tq == 0
    q_blk = 0
    k_blk = GQA_Q_WIDTH // GQA_KV_WIDTH
    trips = jnp.full((1,), seq // (tk * unroll), jnp.int32)
    return pl.pallas_call(
        functools.partial(_gqa_kernel, seq=seq),
        grid=(nb, seq // tq),
        in_specs=[
            pl.BlockSpec(memory_space=pltpu.SMEM),
            pl.BlockSpec((1, tq, GQA_Q_WIDTH), lambda b, i: (b, i, q_blk)),
            pl.BlockSpec((1, seq, GQA_KV_WIDTH), lambda b, i: (b, 0, k_blk)),
            pl.BlockSpec((1, seq, GQA_KV_WIDTH), lambda b, i: (b, 0, k_blk + 1)),
        ],
        out_specs=pl.BlockSpec((1, tq, GQA_Q_WIDTH), lambda b, i: (b, i, 0)),
        out_shape=jax.ShapeDtypeStruct((nb, seq, GQA_Q_WIDTH), jnp.bfloat16),
        scratch_shapes=[
            pltpu.VMEM((GQA_KV_HEADS, seq // tk, LANES, tk), jnp.bfloat16),
            pltpu.VMEM((seq, tq), jnp.float32),
            pltpu.VMEM((seq, tq), jnp.float32),
            pltpu.VMEM((LANES, tq), jnp.float32),
        ],
        compiler_params=pltpu.CompilerParams(dimension_semantics=("parallel", "arbitrary"),
                                             vmem_limit_bytes=VMEM_LIMIT),
        name="gqa_attention",
    )(trips, proj3, proj3, proj3)


def _sigmoid(x):
    return 1.0 / (1.0 + jnp.exp(-x))


def _out_proj_kernel(x_ref, oa_ref, ob_ref, g_ref, scale_ref, shift_ref, gate_ref, wzg_ref, bg_ref,
                     wpa_ref, wpb_ref, wout_ref, fg_ref, o_ref, *, final):
    x = x_ref[...]
    hid = _modulated_norm(x, g_ref[...], scale_ref[0], shift_ref[0])
    zg = jnp.dot(hid.astype(jnp.bfloat16), wzg_ref[...], preferred_element_type=jnp.float32)
    za = zg[:, :NA_WIDTH]
    zb = zg[:, NA_WIDTH:NA_WIDTH + GQA_Q_WIDTH]
    ua = oa_ref[...].astype(jnp.float32) * (za * _sigmoid(za))
    ub = ob_ref[...].astype(jnp.float32) * (zb * _sigmoid(zb))
    ya = jnp.dot(ua.astype(jnp.bfloat16), wpa_ref[...], preferred_element_type=jnp.float32)
    yb = jnp.dot(ub.astype(jnp.bfloat16), wpb_ref[...], preferred_element_type=jnp.float32)
    gates = _sigmoid(zg[:, NA_WIDTH + GQA_Q_WIDTH:] + bg_ref[...])
    merged = gates[:, :D_MODEL] * ya + gates[:, D_MODEL:] * yb
    y = jnp.dot(merged.astype(jnp.bfloat16), wout_ref[...], preferred_element_type=jnp.float32)
    out = x + gate_ref[0] * y
    if final:
        ms = jnp.mean(out * out, axis=-1, keepdims=True)
        out = out * lax.rsqrt(ms + EPS) * fg_ref[...]
    o_ref[...] = out


def _out_projection(x2, oa, ob, norm_g, scale, shift, gate, wzg, b_gate, wpa, wpb, wout, final_g, seq, final):
    tokens = x2.shape[0]
    tm = TOKEN_TILE
    per_seq = seq // tm
    vec = lambda i: (0, 0)
    per_batch = lambda i: (i // per_seq, 0, 0)
    return pl.pallas_call(
        functools.partial(_out_proj_kernel, final=final),
        grid=(tokens // tm,),
        in_specs=[
            pl.BlockSpec((tm, D_MODEL), lambda i: (i, 0)),
            pl.BlockSpec((tm, NA_WIDTH), lambda i: (i, 0)),
            pl.BlockSpec((tm, GQA_Q_WIDTH), lambda i: (i, 0)),
            pl.BlockSpec((1, D_MODEL), vec),
            pl.BlockSpec((1, 1, D_MODEL), per_batch),
            pl.BlockSpec((1, 1, D_MODEL), per_batch),
            pl.BlockSpec((1, 1, D_MODEL), per_batch),
            pl.BlockSpec((D_MODEL, ZG_WIDTH), vec),
            pl.BlockSpec((1, 2 * D_MODEL), vec),
            pl.BlockSpec((NA_WIDTH, D_MODEL), vec),
            pl.BlockSpec((GQA_Q_WIDTH, D_MODEL), vec),
            pl.BlockSpec((D_MODEL, D_MODEL), vec),
            pl.BlockSpec((1, D_MODEL), vec),
        ],
        out_specs=pl.BlockSpec((tm, D_MODEL), lambda i: (i, 0)),
        out_shape=jax.ShapeDtypeStruct((tokens, D_MODEL), jnp.float32),
        compiler_params=pltpu.CompilerParams(dimension_semantics=("parallel",), vmem_limit_bytes=VMEM_LIMIT),
        name="out_projection",
    )(x2, oa, ob, norm_g, scale, shift, gate, wzg, b_gate, wpa, wpb, wout, final_g)


def _rope_tables(seq):
    t = jnp.arange(seq)
    row = (t // GRID_W).astype(jnp.float32)
    col = (t % GRID_W).astype(jnp.float32)
    inv = ROPE_THETA ** (-jnp.arange(0, ROPE_AXIS_DIM, 2, dtype=jnp.float32) / ROPE_AXIS_DIM)
    ang_r, ang_c = row[:, None] * inv, col[:, None] * inv
    cos = jnp.concatenate([jnp.cos(ang_r)] * 2 + [jnp.cos(ang_c)] * 2, axis=-1)
    sin = jnp.concatenate([-jnp.sin(ang_r), jnp.sin(ang_r), -jnp.sin(ang_c), jnp.sin(ang_c)], axis=-1)
    return jnp.tile(cos, (1, 2)), jnp.tile(sin, (1, 2))


def _prepare_layer(l, norm_g, w_in, b_gate, rpb, q_norm_g, k_norm_g, w_pa, w_pb, w_out):
    bounds = np.concatenate([[0], np.cumsum(IN_SPLITS)])
    qa, ka, va, za, qb, kb, vb, zb, ga, gb = (w_in[l][:, bounds[i]:bounds[i + 1]] for i in range(10))
    order = np.concatenate([np.arange(HEAD_DIM) + HEAD_DIM * h for h in _GQA_HEAD_ORDER])
    bf = jnp.bfloat16
    return dict(
        norm_g=norm_g[l][None, :],
        w_gqa=jnp.concatenate([qb[:, order], kb, vb], axis=1).astype(bf),
        w_na=jnp.concatenate([qa * (ATTN_SCALE * LOG2_E), ka, va], axis=1).astype(bf),
        wzg=jnp.concatenate([za, zb[:, order], ga, gb], axis=1).astype(bf),
        b_gate=b_gate[l][None, :],
        bias=_na_bias_tables(rpb[l]),
        qg=jnp.tile(q_norm_g[l] * (ATTN_SCALE * LOG2_E), 2)[None, :],
        kg=jnp.tile(k_norm_g[l], 2)[None, :],
        wpa=w_pa[l].astype(bf),
        wpb=w_pb[l][order, :].astype(bf),
        wout=w_out[l].astype(bf),
    )


def _trunk(x, mod, layers, final_g):
    nb, seq, _ = x.shape
    x2 = x.reshape(nb * seq, D_MODEL)
    cos_t, sin_t = _rope_tables(seq)
    for l, p in enumerate(layers):
        shift, scale, gate = (mod[l][:, None, i * D_MODEL:(i + 1) * D_MODEL] for i in range(3))
        proj_gqa, proj_na = _in_projection(x2, p["norm_g"], scale, shift, p["w_gqa"], p["w_na"], cos_t, sin_t,
                                           p["qg"], p["kg"], seq)
        oa = _na_attention(proj_na.reshape(nb, seq, NA_PROJ_WIDTH), p["bias"]).reshape(nb * seq, NA_WIDTH)
        ob = _gqa_attention(proj_gqa.reshape(nb, seq, GQA_PROJ_WIDTH)).reshape(nb * seq, GQA_Q_WIDTH)
        x2 = _out_projection(x2, oa, ob, p["norm_g"], scale, shift, gate, p["wzg"], p["b_gate"],
                             p["wpa"], p["wpb"], p["wout"], final_g[None, :], seq, final=(l == len(layers) - 1))
    return x2.reshape(nb, seq, D_MODEL)


def kernel(x_prompt, x_sample, c_prompt, c_sample, norm_g, w_ada, b_ada, w_in, b_gate, rpb, q_norm_g, k_norm_g, w_pa, w_pb, w_out, final_g):
    layers = [_prepare_layer(l, norm_g, w_in, b_gate, rpb, q_norm_g, k_norm_g, w_pa, w_pb, w_out)
              for l in range(DEPTH)]
    n_prompt = c_prompt.shape[0]
    mod = _modulation(jnp.concatenate([c_prompt, c_sample], axis=0), w_ada, b_ada)
    y_prompt = _trunk(x_prompt, mod[:, :n_prompt], layers, final_g)
    y_sample = _trunk(x_sample, mod[:, n_prompt:], layers, final_g)
    return (y_prompt, y_sample)
```

```python
import functools

import jax
import jax.numpy as jnp
import numpy as np
from jax import lax
from jax.experimental import pallas as pl
from jax.experimental.pallas import tpu as pltpu

D_MODEL = 1024
DEPTH = 4
GRID_W = 64
HEAD_DIM = 64
NA_HEADS = 8
NA_WIDTH = NA_HEADS * HEAD_DIM
NA_WIN_H = 8
NA_WIN_W = 16
GQA_Q_HEADS = 8
GQA_KV_HEADS = 2
GQA_GROUP = GQA_Q_HEADS // GQA_KV_HEADS
GQA_Q_WIDTH = GQA_Q_HEADS * HEAD_DIM
GQA_KV_WIDTH = GQA_KV_HEADS * HEAD_DIM
ROPE_AXIS_DIM = HEAD_DIM // 2
ROPE_THETA = 10000.0
EPS = 1e-6
ATTN_SCALE = HEAD_DIM ** -0.5
IN_SPLITS = (NA_WIDTH,) * 4 + (GQA_Q_WIDTH, GQA_KV_WIDTH, GQA_KV_WIDTH, GQA_Q_WIDTH, D_MODEL, D_MODEL)
MASKED = -1e30
LOG2_E = 1.4426950408889634

LANES = 128
NA_PROJ_WIDTH = 3 * NA_WIDTH
GQA_PROJ_WIDTH = GQA_Q_WIDTH + 2 * GQA_KV_WIDTH
ZG_WIDTH = NA_WIDTH + GQA_Q_WIDTH + 2 * D_MODEL
TOKEN_TILE = 512
TOKEN_SUBTILES = 2
NA_ROWS = 4
NA_BAND = 12
NA_KEY_TILE = NA_ROWS * GRID_W
GQA_SCORE_BYTES = 16 * 1024 * 1024
GQA_K_TILE = 512
GQA_MAX_UNROLL = 8
VMEM_LIMIT = 56 * 1024 * 1024


def _low_half(shape):
    return lax.broadcasted_iota(jnp.int32, shape, len(shape) - 1) < HEAD_DIM


def _mod_kernel(c_ref, w_ref, b_ref, o_ref):
    c = c_ref[...]
    sc = c * (1.0 / (1.0 + jnp.exp(-c)))
    o_ref[0] = jnp.dot(sc, w_ref[0], precision=lax.Precision.HIGHEST,
                       preferred_element_type=jnp.float32) + b_ref[0]


def _modulation(c_all, w_ada, b_ada):
    nb = c_all.shape[0]
    n_chunks = 3
    return pl.pallas_call(
        _mod_kernel,
        grid=(DEPTH, n_chunks),
        in_specs=[
            pl.BlockSpec((nb, D_MODEL), lambda l, j: (0, 0)),
            pl.BlockSpec((1, D_MODEL, D_MODEL), lambda l, j: (l, 0, j)),
            pl.BlockSpec((1, 1, D_MODEL), lambda l, j: (l, 0, j)),
        ],
        out_specs=pl.BlockSpec((1, nb, D_MODEL), lambda l, j: (l, 0, j)),
        out_shape=jax.ShapeDtypeStruct((DEPTH, nb, 3 * D_MODEL), jnp.float32),
        compiler_params=pltpu.CompilerParams(vmem_limit_bytes=VMEM_LIMIT),
        name="adaln_modulation",
    )(c_all, w_ada, b_ada.reshape(DEPTH, 1, 3 * D_MODEL))


def _modulated_norm(x, g, scale, shift):
    ms = jnp.mean(x * x, axis=-1, keepdims=True)
    y = x * lax.rsqrt(ms + EPS) * g
    return y * (1.0 + scale) + shift


def _head_norm_rope(x, g, cos, sin_signed, low):
    sq = x * x
    lo = jnp.sum(jnp.where(low, sq, 0.0), axis=-1, keepdims=True)
    hi = jnp.sum(jnp.where(low, 0.0, sq), axis=-1, keepdims=True)
    ms = jnp.where(low, lo, hi) * (1.0 / HEAD_DIM)
    y = x * lax.rsqrt(ms + EPS) * g
    lane = lax.broadcasted_iota(jnp.int32, x.shape, 1)
    quarter = ROPE_AXIS_DIM // 2
    partner = jnp.where((lane & quarter) != 0, pltpu.roll(y, quarter, 1), pltpu.roll(y, LANES - quarter, 1))
    return y * cos + partner * sin_signed


def _in_proj_kernel(x_ref, g_ref, scale_ref, shift_ref, wb_ref, wa_ref, cos_ref, sin_ref, qg_ref, kg_ref,
                    ob_ref, oa_ref):
    n_q_chunks = GQA_Q_WIDTH // LANES
    v0 = GQA_Q_WIDTH + GQA_KV_WIDTH
    sub = x_ref.shape[0] // TOKEN_SUBTILES
    low = _low_half((sub, LANES))
    for t in range(TOKEN_SUBTILES):
        rows = slice(t * sub, (t + 1) * sub)
        hid = _modulated_norm(x_ref[rows, :], g_ref[...], scale_ref[0], shift_ref[0]).astype(jnp.bfloat16)
        acc = jnp.dot(hid, wb_ref[...], preferred_element_type=jnp.float32)
        cos = cos_ref[rows, :]
        sin = sin_ref[rows, :]
        for j in range(n_q_chunks + 1):
            c0 = j * LANES
            gain = qg_ref[...] if j < n_q_chunks else kg_ref[...]
            ob_ref[rows, c0:c0 + LANES] = _head_norm_rope(acc[:, c0:c0 + LANES], gain, cos, sin,
                                                          low).astype(ob_ref.dtype)
        ob_ref[rows, v0:] = acc[:, v0:].astype(ob_ref.dtype)
        oa_ref[rows, :] = jnp.dot(hid, wa_ref[...], preferred_element_type=jnp.float32).astype(oa_ref.dtype)


def _in_projection(x2, norm_g, scale, shift, wb, wa, cos_t, sin_t, qg, kg, seq):
    tokens = x2.shape[0]
    tm = TOKEN_TILE
    per_seq = seq // tm
    vec = lambda i: (0, 0)
    return pl.pallas_call(
        _in_proj_kernel,
        grid=(tokens // tm,),
        in_specs=[
            pl.BlockSpec((tm, D_MODEL), lambda i: (i, 0)),
            pl.BlockSpec((1, D_MODEL), vec),
            pl.BlockSpec((1, 1, D_MODEL), lambda i: (i // per_seq, 0, 0)),
            pl.BlockSpec((1, 1, D_MODEL), lambda i: (i // per_seq, 0, 0)),
            pl.BlockSpec((D_MODEL, GQA_PROJ_WIDTH), vec),
            pl.BlockSpec((D_MODEL, NA_PROJ_WIDTH), vec),
            pl.BlockSpec((tm, LANES), lambda i: (i % per_seq, 0)),
            pl.BlockSpec((tm, LANES), lambda i: (i % per_seq, 0)),
            pl.BlockSpec((1, LANES), vec),
            pl.BlockSpec((1, LANES), vec),
        ],
        out_specs=[pl.BlockSpec((tm, GQA_PROJ_WIDTH), lambda i: (i, 0)),
                   pl.BlockSpec((tm, NA_PROJ_WIDTH), lambda i: (i, 0))],
        out_shape=[jax.ShapeDtypeStruct((tokens, GQA_PROJ_WIDTH), jnp.bfloat16),
                   jax.ShapeDtypeStruct((tokens, NA_PROJ_WIDTH), jnp.bfloat16)],
        compiler_params=pltpu.CompilerParams(dimension_semantics=("parallel",), vmem_limit_bytes=VMEM_LIMIT),
        name="in_projection",
    )(x2, norm_g, scale, shift, wb, wa, cos_t, sin_t, qg, kg)


def _na_kernel(q_ref, k_ref, v_ref, bias_ref, o_ref, vt_ref, s0_ref, s1_ref, *, grid_rows):
    g = pl.program_id(1)
    seq = grid_rows * GRID_W
    nk = NA_BAND * GRID_W
    nq = NA_ROWS * GRID_W
    kt = NA_KEY_TILE
    s_refs = (s0_ref, s1_ref)

    @pl.when(g == 0)
    def _():
        row = lax.broadcasted_iota(jnp.int32, (NA_WIDTH, kt), 0)
        even_rows = (row & HEAD_DIM) == 0
        for j in range(seq // kt):
            t = v_ref[0, j * kt:(j + 1) * kt, :].astype(jnp.float32).T
            vt_ref[0, j] = jnp.where(even_rows, t, 1.0).astype(jnp.bfloat16)
            vt_ref[1, j] = jnp.where(even_rows, 1.0, t).astype(jnp.bfloat16)

    band_row = jnp.clip(NA_ROWS * g - NA_WIN_H // 2, 0, grid_rows - NA_BAND)
    k0 = pl.multiple_of(band_row * GRID_W, kt)
    kb = band_row // (kt // GRID_W)
    low = _low_half((nq, LANES))

    def scores(c):
        qc = q_ref[0, :, c * LANES:(c + 1) * LANES].astype(jnp.float32)
        qm = jnp.concatenate([jnp.where(low, qc, 0.0), jnp.where(low, 0.0, qc)], axis=0).astype(jnp.bfloat16)
        s = lax.dot_general(k_ref[0, pl.ds(k0, nk), c * LANES:(c + 1) * LANES], qm, (((1,), (1,)), ((), ())),
                            preferred_element_type=jnp.float32)
        s = s + bias_ref[0, c]
        s_refs[c % 2][...] = s
        return jnp.max(jnp.max(s.reshape(nk // 8, 8, 2 * nq), axis=0), axis=0, keepdims=True)

    def weighted(c, m):
        accs = [jnp.zeros((LANES, nq), jnp.float32) for _ in range(2)]
        for t in range(nk // kt):
            p = jnp.exp2(s_refs[c % 2][t * kt:(t + 1) * kt, :] - m).astype(jnp.bfloat16)
            for half in range(2):
                accs[half] = accs[half] + jnp.dot(vt_ref[half, kb + t, c * LANES:(c + 1) * LANES, :],
                                                  p[:, half * nq:(half + 1) * nq],
                                                  preferred_element_type=jnp.float32)
        halves = []
        for half in range(2):
            other = (1 - half) * HEAD_DIM
            halves.append(accs[half][half * HEAD_DIM:(half + 1) * HEAD_DIM] / accs[half][other:other + 1])
        o_ref[0, :, c * LANES:(c + 1) * LANES] = jnp.concatenate(halves, axis=0).T.astype(o_ref.dtype)

    n_pairs = NA_WIDTH // LANES
    m = scores(0)
    for c in range(1, n_pairs + 1):
        m_next = scores(c) if c < n_pairs else None
        weighted(c - 1, m)
        m = m_next


def _na_attention(proj3, bias):
    nb, seq, _ = proj3.shape
    grid_rows = seq // GRID_W
    assert grid_rows % NA_ROWS == 0 and grid_rows >= NA_BAND
    steps = grid_rows // NA_ROWS
    nq = NA_ROWS * GRID_W
    variant = lambda b, g: ((g > 0).astype(jnp.int32) + (g == steps - 1).astype(jnp.int32), 0, 0, 0)
    return pl.pallas_call(
        functools.partial(_na_kernel, grid_rows=grid_rows),
        grid=(nb, steps),
        in_specs=[
            pl.BlockSpec((1, nq, NA_WIDTH), lambda b, g: (b, g, 0)),
            pl.BlockSpec((1, seq, NA_WIDTH), lambda b, g: (b, 0, 1)),
            pl.BlockSpec((1, seq, NA_WIDTH), lambda b, g: (b, 0, 2)),
            pl.BlockSpec((1, NA_HEADS // 2, NA_BAND * GRID_W, 2 * nq), variant),
        ],
        out_specs=pl.BlockSpec((1, nq, NA_WIDTH), lambda b, g: (b, g, 0)),
        out_shape=jax.ShapeDtypeStruct((nb, seq, NA_WIDTH), jnp.bfloat16),
        scratch_shapes=[
            pltpu.VMEM((2, seq // NA_KEY_TILE, NA_WIDTH, NA_KEY_TILE), jnp.bfloat16),
            pltpu.VMEM((NA_BAND * GRID_W, 2 * nq), jnp.float32),
            pltpu.VMEM((NA_BAND * GRID_W, 2 * nq), jnp.float32),
        ],
        compiler_params=pltpu.CompilerParams(dimension_semantics=("parallel", "arbitrary"),
                                             vmem_limit_bytes=VMEM_LIMIT),
        name="neighbourhood_attention",
    )(proj3, proj3, proj3, bias)


def _na_bias_tables(rpb_l):
    i = np.arange(NA_ROWS)[:, None]
    a = np.arange(NA_BAND)[None, :]
    d = a - i
    start = np.stack([np.zeros_like(d), np.zeros_like(d) + i, np.zeros_like(d) + 4])
    row_ok = (a[None] >= start) & (a[None] < start + NA_WIN_H)
    row_off = np.stack([d + 7, d + 3, d - 1])
    row_off = np.clip(row_off, 0, 2 * NA_WIN_H - 2)
    cq = np.arange(GRID_W)[:, None]
    ck = np.arange(GRID_W)[None, :]
    c0 = np.clip(cq - NA_WIN_W // 2, 0, GRID_W - NA_WIN_W)
    col_ok = (ck >= c0) & (ck < c0 + NA_WIN_W)
    col_off = np.clip(ck - cq + NA_WIN_W - 1, 0, 2 * NA_WIN_W - 2)
    cols = jnp.where(col_ok.T[None, None], rpb_l.astype(jnp.float32)[:, :, col_off.T] * LOG2_E, MASKED)
    t = jnp.take(cols, row_off.reshape(-1), axis=1).reshape(NA_HEADS // 2, 2, 3, NA_ROWS, NA_BAND, GRID_W, GRID_W)
    t = jnp.where(row_ok[None, None, :, :, :, None, None], t, MASKED)
    t = jnp.transpose(t, (2, 0, 4, 5, 1, 3, 6))
    return t.reshape(3, NA_HEADS // 2, NA_BAND * GRID_W, 2 * NA_ROWS * GRID_W)


def _gqa_kernel(trips_ref, q_ref, k_ref, v_ref, o_ref, vt_ref, s0_ref, s1_ref, acc_ref, *, seq):
    tq, tk = _gqa_q_tile(seq), GQA_K_TILE
    n_chunks = seq // tk
    unroll = _gqa_unroll(seq)
    s_refs = (s0_ref, s1_ref)
    MAX_ROWS = 8

    @pl.when(pl.program_id(1) == 0)
    def _():
        ones = jnp.ones((HEAD_DIM, tk), jnp.bfloat16)
        for j in range(n_chunks):
            t = v_ref[0, j * tk:(j + 1) * tk, :].astype(jnp.float32).T
            for hk in range(GQA_KV_HEADS):
                vt_ref[hk, j, :HEAD_DIM, :] = t[hk * HEAD_DIM:(hk + 1) * HEAD_DIM].astype(jnp.bfloat16)
                vt_ref[hk, j, HEAD_DIM:, :] = ones

    low = _low_half((tq, LANES))
    heads = [(c, hk) for c in range(GQA_Q_WIDTH // LANES) for hk in range(GQA_KV_HEADS)]

    def masked_q(c, hk):
        qc = q_ref[0, :, c * LANES:(c + 1) * LANES].astype(jnp.float32)
        own = low if hk == 0 else jnp.logical_not(low)
        return jnp.where(own, qc, 0.0).astype(jnp.bfloat16)

    def scores(j, qm, slot, mx):
        r0 = pl.multiple_of(j * tk, tk)
        s = lax.dot_general(k_ref[0, pl.ds(r0, tk), :], qm, (((1,), (1,)), ((), ())),
                            preferred_element_type=jnp.float32)
        s_refs[slot][pl.ds(r0, tk), :] = s
        return jnp.maximum(mx, jnp.max(s.reshape(tk // MAX_ROWS, MAX_ROWS, tq), axis=0))

    def weighted(j, hk, slot, m):
        r0 = pl.multiple_of(j * tk, tk)
        p = jnp.exp2(s_refs[slot][pl.ds(r0, tk), :] - m).astype(jnp.bfloat16)
        acc_ref[...] += jnp.dot(vt_ref[hk, j], p, preferred_element_type=jnp.float32)

    def chunk_loop(step, init):
        def body(t, carry):
            for u in range(unroll):
                carry = step(t * unroll + u, carry)
            return carry
        return lax.fori_loop(0, trips_ref[0], body, init)

    mx0 =jnp.full((MAX_ROWS, tq), -jnp.inf, jnp.float32)
    qm = masked_q(*heads[0])
    mx = chunk_loop(lambda j, mx, qm=qm: scores(j, qm, 0, mx), mx0)
    halves = []
    for i in range(1, len(heads) + 1):
        m = jnp.max(mx, axis=0, keepdims=True)
        c_prev, hk_prev = heads[i - 1]
        cur = (i - 1) % 2
        acc_ref[...] = jnp.zeros_like(acc_ref)
        if i < len(heads):
            qm = masked_q(*heads[i])

            def both(j, mx, qm=qm, m=m, hk_prev=hk_prev, cur=cur):
                weighted(j, hk_prev, cur, m)
                return scores(j, qm, 1 - cur, mx)

            mx = chunk_loop(both, mx0)
        else:
            def last(j, carry, m=m, hk_prev=hk_prev, cur=cur):
                weighted(j, hk_prev, cur, m)
                return carry

            chunk_loop(last, 0)
        acc = acc_ref[...]
        halves.append(acc[:HEAD_DIM] / acc[HEAD_DIM:HEAD_DIM + 1])
        if hk_prev == GQA_KV_HEADS - 1:
            o_ref[0, :, c_prev * LANES:(c_prev + 1) * LANES] = jnp.concatenate(halves, axis=0).T.astype(o_ref.dtype)
            halves = []


def _gqa_unroll(seq):
    return min(GQA_MAX_UNROLL, seq // GQA_K_TILE)


def _gqa_q_tile(seq):
    return min(seq, GQA_SCORE_BYTES // (2 * 4 * seq))


def _gqa_attention(proj3):
    nb, seq, _ = proj3.shape
    tq, tk = _gqa_q_tile(seq), GQA_K_TILE
    unroll = _gqa_unroll(seq)
    assert seq % (tk * unroll) == 0 and seq % tq == 0
    q_blk = 0
    k_blk = GQA_Q_WIDTH // GQA_KV_WIDTH
    trips = jnp.full((1,), seq // (tk * unroll), jnp.int32)
    return pl.pallas_call(
        functools.partial(_gqa_kernel, seq=seq),
        grid=(nb, seq // tq),
        in_specs=[
            pl.BlockSpec(memory_space=pltpu.SMEM),
            pl.BlockSpec((1, tq, GQA_Q_WIDTH), lambda b, i: (b, i, q_blk)),
            pl.BlockSpec((1, seq, GQA_KV_WIDTH), lambda b, i: (b, 0, k_blk)),
            pl.BlockSpec((1, seq, GQA_KV_WIDTH), lambda b, i: (b, 0, k_blk + 1)),
        ],
        out_specs=pl.BlockSpec((1, tq, GQA_Q_WIDTH), lambda b, i: (b, i, 0)),
        out_shape=jax.ShapeDtypeStruct((nb, seq, GQA_Q_WIDTH), jnp.bfloat16),
        scratch_shapes=[
            pltpu.VMEM((GQA_KV_HEADS, seq // tk, LANES, tk), jnp.bfloat16),
            pltpu.VMEM((seq, tq), jnp.float32),
            pltpu.VMEM((seq, tq), jnp.float32),
            pltpu.VMEM((LANES, tq), jnp.float32),
        ],
        compiler_params=pltpu.CompilerParams(dimension_semantics=("parallel", "arbitrary"),
                                             vmem_limit_bytes=VMEM_LIMIT),
        name="gqa_attention",
    )(trips, proj3, proj3, proj3)


def _sigmoid(x):
    return 1.0 / (1.0 + jnp.exp(-x))


def _out_proj_kernel(x_ref, oa_ref, ob_ref, g_ref, scale_ref, shift_ref, gate_ref, wzg_ref, bg_ref,
                     wpa_ref, wpb_ref, wout_ref, fg_ref, o_ref, *, final):
    sub = x_ref.shape[0] // TOKEN_SUBTILES
    for t in range(TOKEN_SUBTILES):
        rows = slice(t * sub, (t + 1) * sub)
        x = x_ref[rows, :]
        hid = _modulated_norm(x, g_ref[...], scale_ref[0], shift_ref[0])
        zg = jnp.dot(hid.astype(jnp.bfloat16), wzg_ref[...], preferred_element_type=jnp.float32)
        za = zg[:, :NA_WIDTH]
        zb = zg[:, NA_WIDTH:NA_WIDTH + GQA_Q_WIDTH]
        ua = oa_ref[rows, :].astype(jnp.float32) * (za * _sigmoid(za))
        ub = ob_ref[rows, :].astype(jnp.float32) * (zb * _sigmoid(zb))
        ya = jnp.dot(ua.astype(jnp.bfloat16), wpa_ref[...], preferred_element_type=jnp.float32)
        yb = jnp.dot(ub.astype(jnp.bfloat16), wpb_ref[...], preferred_element_type=jnp.float32)
        gates = _sigmoid(zg[:, NA_WIDTH + GQA_Q_WIDTH:] + bg_ref[...])
        merged = gates[:, :D_MODEL] * ya + gates[:, D_MODEL:] * yb
        y = jnp.dot(merged.astype(jnp.bfloat16), wout_ref[...], preferred_element_type=jnp.float32)
        out = x + gate_ref[0] * y
        if final:
            ms = jnp.mean(out * out, axis=-1, keepdims=True)
            out = out * lax.rsqrt(ms + EPS) * fg_ref[...]
        o_ref[rows, :] = out


def _out_projection(x2, oa, ob, norm_g, scale, shift, gate, wzg, b_gate, wpa, wpb, wout, final_g, seq, final):
    tokens = x2.shape[0]
    tm = TOKEN_TILE
    per_seq = seq // tm
    vec = lambda i: (0, 0)
    per_batch = lambda i: (i // per_seq, 0, 0)
    return pl.pallas_call(
        functools.partial(_out_proj_kernel, final=final),
        grid=(tokens // tm,),
        in_specs=[
            pl.BlockSpec((tm, D_MODEL), lambda i: (i, 0)),
            pl.BlockSpec((tm, NA_WIDTH), lambda i: (i, 0)),
            pl.BlockSpec((tm, GQA_Q_WIDTH), lambda i: (i, 0)),
            pl.BlockSpec((1, D_MODEL), vec),
            pl.BlockSpec((1, 1, D_MODEL), per_batch),
            pl.BlockSpec((1, 1, D_MODEL), per_batch),
            pl.BlockSpec((1, 1, D_MODEL), per_batch),
            pl.BlockSpec((D_MODEL, ZG_WIDTH), vec),
            pl.BlockSpec((1, 2 * D_MODEL), vec),
            pl.BlockSpec((NA_WIDTH, D_MODEL), vec),
            pl.BlockSpec((GQA_Q_WIDTH, D_MODEL), vec),
            pl.BlockSpec((D_MODEL, D_MODEL), vec),
            pl.BlockSpec((1, D_MODEL), vec),
        ],
        out_specs=pl.BlockSpec((tm, D_MODEL), lambda i: (i, 0)),
        out_shape=jax.ShapeDtypeStruct((tokens, D_MODEL), jnp.float32),
        compiler_params=pltpu.CompilerParams(dimension_semantics=("parallel",), vmem_limit_bytes=VMEM_LIMIT),
        name="out_projection",
    )(x2, oa, ob, norm_g, scale, shift, gate, wzg, b_gate, wpa, wpb, wout, final_g)


def _rope_tables(seq):
    t = jnp.arange(seq)
    row = (t // GRID_W).astype(jnp.float32)
    col = (t % GRID_W).astype(jnp.float32)
    inv = ROPE_THETA ** (-jnp.arange(0, ROPE_AXIS_DIM, 2, dtype=jnp.float32) / ROPE_AXIS_DIM)
    ang_r, ang_c = row[:, None] * inv, col[:, None] * inv
    cos = jnp.concatenate([jnp.cos(ang_r)] * 2 + [jnp.cos(ang_c)] * 2, axis=-1)
    sin = jnp.concatenate([-jnp.sin(ang_r), jnp.sin(ang_r), -jnp.sin(ang_c), jnp.sin(ang_c)], axis=-1)
    return jnp.tile(cos, (1, 2)), jnp.tile(sin, (1, 2))


def _prepare_layer(l, norm_g, w_in, b_gate, rpb, q_norm_g, k_norm_g, w_pa, w_pb, w_out):
    bounds = np.concatenate([[0], np.cumsum(IN_SPLITS)])
    qa, ka, va, za, qb, kb, vb, zb, ga, gb = (w_in[l][:, bounds[i]:bounds[i + 1]] for i in range(10))
    bf = jnp.bfloat16

    def pair_heads(w):
        r = w.shape[0]
        w = w.reshape(r, GQA_KV_HEADS, GQA_GROUP, HEAD_DIM)
        return jnp.transpose(w, (0, 2, 1, 3)).reshape(r, GQA_Q_WIDTH)

    return dict(
        norm_g=norm_g[l][None, :],
        w_gqa=jnp.concatenate([pair_heads(qb), kb, vb], axis=1).astype(bf),
        w_na=jnp.concatenate([qa * (ATTN_SCALE * LOG2_E), ka, va], axis=1).astype(bf),
        wzg=jnp.concatenate([za, pair_heads(zb), ga, gb], axis=1).astype(bf),
        b_gate=b_gate[l][None, :],
        bias=_na_bias_tables(rpb[l]),
        qg=jnp.tile(q_norm_g[l] * (ATTN_SCALE * LOG2_E), 2)[None, :],
        kg=jnp.tile(k_norm_g[l], 2)[None, :],
        wpa=w_pa[l].astype(bf),
        wpb=jnp.transpose(w_pb[l].reshape(GQA_KV_HEADS, GQA_GROUP, HEAD_DIM, D_MODEL),
                          (1, 0, 2, 3)).reshape(GQA_Q_WIDTH, D_MODEL).astype(bf),
        wout=w_out[l].astype(bf),
    )


def _trunk(x, mod, layers, final_g):
    nb, seq, _ = x.shape
    x2 = x.reshape(nb * seq, D_MODEL)
    cos_t, sin_t = _rope_tables(seq)
    for l, p in enumerate(layers):
        shift, scale, gate = (mod[l][:, None, i * D_MODEL:(i + 1) * D_MODEL] for i in range(3))
        proj_gqa, proj_na = _in_projection(x2, p["norm_g"], scale, shift, p["w_gqa"], p["w_na"], cos_t, sin_t,
                                           p["qg"], p["kg"], seq)
        oa = _na_attention(proj_na.reshape(nb, seq, NA_PROJ_WIDTH), p["bias"]).reshape(nb * seq, NA_WIDTH)
        ob = _gqa_attention(proj_gqa.reshape(nb, seq, GQA_PROJ_WIDTH)).reshape(nb * seq, GQA_Q_WIDTH)
        x2 = _out_projection(x2, oa, ob, p["norm_g"], scale, shift, gate, p["wzg"], p["b_gate"],
                             p["wpa"], p["wpb"], p["wout"], final_g[None, :], seq, final=(l == len(layers) - 1))
    return x2.reshape(nb, seq, D_MODEL)


def kernel(x_prompt, x_sample, c_prompt, c_sample, norm_g, w_ada, b_ada, w_in, b_gate, rpb, q_norm_g, k_norm_g, w_pa, w_pb, w_out, final_g):
    layers = [_prepare_layer(l, norm_g, w_in, b_gate, rpb, q_norm_g, k_norm_g, w_pa, w_pb, w_out)
              for l in range(DEPTH)]
    n_prompt = c_prompt.shape[0]
    mod = _modulation(jnp.concatenate([c_prompt, c_sample], axis=0), w_ada, b_ada)
    y_prompt = _trunk(x_prompt, mod[:, :n_prompt], layers, final_g)
    y_sample = _trunk(x_sample, mod[:, n_prompt:], layers, final_g)
    return (y_prompt, y_sample)
```

```python
import functools

import jax
import jax.numpy as jnp
import numpy as np
from jax import lax
from jax.experimental import pallas as pl
from jax.experimental.pallas import tpu as pltpu

D_MODEL = 1024
DEPTH = 4
GRID_W = 64
HEAD_DIM = 64
NA_HEADS = 8
NA_WIDTH = NA_HEADS * HEAD_DIM
NA_WIN_H = 8
NA_WIN_W = 16
GQA_Q_HEADS = 8
GQA_KV_HEADS = 2
GQA_GROUP = GQA_Q_HEADS // GQA_KV_HEADS
GQA_Q_WIDTH = GQA_Q_HEADS * HEAD_DIM
GQA_KV_WIDTH = GQA_KV_HEADS * HEAD_DIM
ROPE_AXIS_DIM = HEAD_DIM // 2
ROPE_THETA = 10000.0
EPS = 1e-6
ATTN_SCALE = HEAD_DIM ** -0.5
IN_SPLITS = (NA_WIDTH,) * 4 + (GQA_Q_WIDTH, GQA_KV_WIDTH, GQA_KV_WIDTH, GQA_Q_WIDTH, D_MODEL, D_MODEL)
MASKED = -1e30
LOG2_E = 1.4426950408889634

LANES = 128
NA_PROJ_WIDTH = 3 * NA_WIDTH
GQA_PROJ_WIDTH = GQA_Q_WIDTH + 2 * GQA_KV_WIDTH
ZG_WIDTH = NA_WIDTH + GQA_Q_WIDTH + 2 * D_MODEL
TOKEN_TILE = 512
TOKEN_SUBTILES = 2
NA_ROWS = 4
NA_BAND = 12
NA_KEY_TILE = NA_ROWS * GRID_W
NA_SUBSTEPS = 2
GQA_SCORE_BYTES = 16 * 1024 * 1024
GQA_K_TILE = 512
GQA_MAX_UNROLL = 8
VMEM_LIMIT = 56 * 1024 * 1024


def _low_half(shape):
    return lax.broadcasted_iota(jnp.int32, shape, len(shape) - 1) < HEAD_DIM


def _mod_kernel(c_ref, w_ref, b_ref, o_ref):
    c = c_ref[...]
    sc = c * (1.0 / (1.0 + jnp.exp(-c)))
    o_ref[0] = jnp.dot(sc, w_ref[0], precision=lax.Precision.HIGHEST,
                       preferred_element_type=jnp.float32) + b_ref[0]


def _modulation(c_all, w_ada, b_ada):
    nb = c_all.shape[0]
    n_chunks = 3
    return pl.pallas_call(
        _mod_kernel,
        grid=(DEPTH, n_chunks),
        in_specs=[
            pl.BlockSpec((nb, D_MODEL), lambda l, j: (0, 0)),
            pl.BlockSpec((1, D_MODEL, D_MODEL), lambda l, j: (l, 0, j)),
            pl.BlockSpec((1, 1, D_MODEL), lambda l, j: (l, 0, j)),
        ],
        out_specs=pl.BlockSpec((1, nb, D_MODEL), lambda l, j: (l, 0, j)),
        out_shape=jax.ShapeDtypeStruct((DEPTH, nb, 3 * D_MODEL), jnp.float32),
        compiler_params=pltpu.CompilerParams(vmem_limit_bytes=VMEM_LIMIT),
        name="adaln_modulation",
    )(c_all, w_ada, b_ada.reshape(DEPTH, 1, 3 * D_MODEL))


def _modulated_norm(x, g, scale, shift):
    ms = jnp.mean(x * x, axis=-1, keepdims=True)
    y = x * lax.rsqrt(ms + EPS) * g
    return y * (1.0 + scale) + shift


def _head_norm_rope(x, g, cos, sin_signed, low):
    sq = x * x
    lo = jnp.sum(jnp.where(low, sq, 0.0), axis=-1, keepdims=True)
    hi = jnp.sum(jnp.where(low, 0.0, sq), axis=-1, keepdims=True)
    ms = jnp.where(low, lo, hi) * (1.0 / HEAD_DIM)
    y = x * lax.rsqrt(ms + EPS) * g
    lane = lax.broadcasted_iota(jnp.int32, x.shape, 1)
    quarter = ROPE_AXIS_DIM // 2
    partner = jnp.where((lane & quarter) != 0, pltpu.roll(y, quarter, 1), pltpu.roll(y, LANES - quarter, 1))
    return y * cos + partner * sin_signed


def _in_proj_kernel(x_ref, g_ref, scale_ref, shift_ref, wb_ref, wa_ref, cos_ref, sin_ref, qg_ref, kg_ref,
                    ob_ref, oa_ref):
    n_q_chunks = GQA_Q_WIDTH // LANES
    v0 = GQA_Q_WIDTH + GQA_KV_WIDTH
    sub = x_ref.shape[0] // TOKEN_SUBTILES
    low = _low_half((sub, LANES))
    for t in range(TOKEN_SUBTILES):
        rows = slice(t * sub, (t + 1) * sub)
        hid = _modulated_norm(x_ref[rows, :], g_ref[...], scale_ref[0], shift_ref[0]).astype(jnp.bfloat16)
        acc = jnp.dot(hid, wb_ref[...], preferred_element_type=jnp.float32)
        cos = cos_ref[rows, :]
        sin = sin_ref[rows, :]
        for j in range(n_q_chunks + 1):
            c0 = j * LANES
            gain = qg_ref[...] if j < n_q_chunks else kg_ref[...]
            ob_ref[rows, c0:c0 + LANES] = _head_norm_rope(acc[:, c0:c0 + LANES], gain, cos, sin,
                                                          low).astype(ob_ref.dtype)
        ob_ref[rows, v0:] = acc[:, v0:].astype(ob_ref.dtype)
        oa_ref[rows, :] = jnp.dot(hid, wa_ref[...], preferred_element_type=jnp.float32).astype(oa_ref.dtype)


def _in_projection(x2, norm_g, scale, shift, wb, wa, cos_t, sin_t, qg, kg, seq):
    tokens = x2.shape[0]
    tm = TOKEN_TILE
    per_seq = seq // tm
    vec = lambda i: (0, 0)
    return pl.pallas_call(
        _in_proj_kernel,
        grid=(tokens // tm,),
        in_specs=[
            pl.BlockSpec((tm, D_MODEL), lambda i: (i, 0)),
            pl.BlockSpec((1, D_MODEL), vec),
            pl.BlockSpec((1, 1, D_MODEL), lambda i: (i // per_seq, 0, 0)),
            pl.BlockSpec((1, 1, D_MODEL), lambda i: (i // per_seq, 0, 0)),
            pl.BlockSpec((D_MODEL, GQA_PROJ_WIDTH), vec),
            pl.BlockSpec((D_MODEL, NA_PROJ_WIDTH), vec),
            pl.BlockSpec((tm, LANES), lambda i: (i % per_seq, 0)),
            pl.BlockSpec((tm, LANES), lambda i: (i % per_seq, 0)),
            pl.BlockSpec((1, LANES), vec),
            pl.BlockSpec((1, LANES), vec),
        ],
        out_specs=[pl.BlockSpec((tm, GQA_PROJ_WIDTH), lambda i: (i, 0)),
                   pl.BlockSpec((tm, NA_PROJ_WIDTH), lambda i: (i, 0))],
        out_shape=[jax.ShapeDtypeStruct((tokens, GQA_PROJ_WIDTH), jnp.bfloat16),
                   jax.ShapeDtypeStruct((tokens, NA_PROJ_WIDTH), jnp.bfloat16)],
        compiler_params=pltpu.CompilerParams(dimension_semantics=("parallel",), vmem_limit_bytes=VMEM_LIMIT),
        name="in_projection",
    )(x2, norm_g, scale, shift, wb, wa, cos_t, sin_t, qg, kg)


def _na_kernel(q_ref, k_ref, v_ref, bias0_ref, bias1_ref, o_ref, vt_ref, *, grid_rows):
    step = pl.program_id(1)
    seq = grid_rows * GRID_W
    nk = NA_BAND * GRID_W
    nq = NA_ROWS * GRID_W
    kt = NA_KEY_TILE
    bias_refs = (bias0_ref, bias1_ref)

    @pl.when(step == 0)
    def _():
        row = lax.broadcasted_iota(jnp.int32, (NA_WIDTH, kt), 0)
        even_rows = (row & HEAD_DIM) == 0
        for j in range(seq // kt):
            t = v_ref[0, j * kt:(j + 1) * kt, :].astype(jnp.float32).T
            vt_ref[0, j] = jnp.where(even_rows, t, 1.0).astype(jnp.bfloat16)
            vt_ref[1, j] = jnp.where(even_rows, 1.0, t).astype(jnp.bfloat16)

    low = _low_half((nq, LANES))
    band_rows = [jnp.clip(NA_ROWS * (NA_SUBSTEPS * step + sub) - NA_WIN_H // 2, 0, grid_rows - NA_BAND)
                 for sub in range(NA_SUBSTEPS)]

    def scores(sub, c):
        k0 = pl.multiple_of(band_rows[sub] * GRID_W, kt)
        qc = q_ref[0, sub * nq:(sub + 1) * nq, c * LANES:(c + 1) * LANES].astype(jnp.float32)
        qm = jnp.concatenate([jnp.where(low, qc, 0.0), jnp.where(low, 0.0, qc)], axis=0).astype(jnp.bfloat16)
        s = lax.dot_general(k_ref[0, pl.ds(k0, nk), c * LANES:(c + 1) * LANES], qm, (((1,), (1,)), ((), ())),
                            preferred_element_type=jnp.float32)
        s = s + bias_refs[sub][0, c]
        return s, jnp.max(jnp.max(s.reshape(nk // 8, 8, 2 * nq), axis=0), axis=0, keepdims=True)

    def weighted(sub, c, s, m):
        kb = band_rows[sub] // (kt // GRID_W)
        accs = [jnp.zeros((LANES, nq), jnp.float32) for _ in range(2)]
        for t in range(nk // kt):
            p = jnp.exp2(s[t * kt:(t + 1) * kt] - m).astype(jnp.bfloat16)
            for half in range(2):
                accs[half] = accs[half] + jnp.dot(vt_ref[half, kb + t, c * LANES:(c + 1) * LANES, :],
                                                  p[:, half * nq:(half + 1) * nq],
                                                  preferred_element_type=jnp.float32)
        halves = []
        for half in range(2):
            other = (1 - half) * HEAD_DIM
            halves.append(accs[half][half * HEAD_DIM:(half + 1) * HEAD_DIM] / accs[half][other:other + 1])
        o_ref[0, sub * nq:(sub + 1) * nq, c * LANES:(c + 1) * LANES] = (
            jnp.concatenate(halves, axis=0).T.astype(o_ref.dtype))

    units = [(sub, c) for c in range(NA_WIDTH // LANES) for sub in range(NA_SUBSTEPS)]
    ahead = NA_SUBSTEPS
    pending = [scores(*u) for u in units[:ahead]]
    for n, u in enumerate(units):
        if n + ahead < len(units):
            pending.append(scores(*units[n + ahead]))
        weighted(*u, *pending.pop(0))


def _na_attention(proj3, bias):
    nb, seq, _ = proj3.shape
    grid_rows = seq // GRID_W
    assert grid_rows % (NA_ROWS * NA_SUBSTEPS) == 0 and grid_rows >= NA_BAND and NA_SUBSTEPS == 2
    steps = grid_rows // (NA_ROWS * NA_SUBSTEPS)
    nq = NA_ROWS * GRID_W
    variant0 = lambda b, g: ((g > 0).astype(jnp.int32), 0, 0, 0)
    variant1 = lambda b, g: (1 + (g == steps - 1).astype(jnp.int32), 0, 0, 0)
    bias_block = (1, NA_HEADS // 2, NA_BAND * GRID_W, 2 * nq)
    return pl.pallas_call(
        functools.partial(_na_kernel, grid_rows=grid_rows),
        grid=(nb, steps),
        in_specs=[
            pl.BlockSpec((1, NA_SUBSTEPS * nq, NA_WIDTH), lambda b, g: (b, g, 0)),
            pl.BlockSpec((1, seq, NA_WIDTH), lambda b, g: (b, 0, 1)),
            pl.BlockSpec((1, seq, NA_WIDTH), lambda b, g: (b, 0, 2)),
            pl.BlockSpec(bias_block, variant0),
            pl.BlockSpec(bias_block, variant1),
        ],
        out_specs=pl.BlockSpec((1, NA_SUBSTEPS * nq, NA_WIDTH), lambda b, g: (b, g, 0)),
        out_shape=jax.ShapeDtypeStruct((nb, seq, NA_WIDTH), jnp.bfloat16),
        scratch_shapes=[pltpu.VMEM((2, seq // NA_KEY_TILE, NA_WIDTH, NA_KEY_TILE), jnp.bfloat16)],
        compiler_params=pltpu.CompilerParams(dimension_semantics=("parallel", "arbitrary"),
                                             vmem_limit_bytes=VMEM_LIMIT),
        name="neighbourhood_attention",
    )(proj3, proj3, proj3, bias, bias)


def _na_bias_tables(rpb_l):
    i = np.arange(NA_ROWS)[:, None]
    a = np.arange(NA_BAND)[None, :]
    d = a - i
    start = np.stack([np.zeros_like(d), np.zeros_like(d) + i, np.zeros_like(d) + 4])
    row_ok = (a[None] >= start) & (a[None] < start + NA_WIN_H)
    row_off = np.stack([d + 7, d + 3, d - 1])
    row_off = np.clip(row_off, 0, 2 * NA_WIN_H - 2)
    cq = np.arange(GRID_W)[:, None]
    ck = np.arange(GRID_W)[None, :]
    c0 = np.clip(cq - NA_WIN_W // 2, 0, GRID_W - NA_WIN_W)
    col_ok = (ck >= c0) & (ck < c0 + NA_WIN_W)
    col_off = np.clip(ck - cq + NA_WIN_W - 1, 0, 2 * NA_WIN_W - 2)
    cols = jnp.where(col_ok.T[None, None], rpb_l.astype(jnp.float32)[:, :, col_off.T] * LOG2_E, MASKED)
    t = jnp.take(cols, row_off.reshape(-1), axis=1).reshape(NA_HEADS // 2, 2, 3, NA_ROWS, NA_BAND, GRID_W, GRID_W)
    t = jnp.where(row_ok[None, None, :, :, :, None, None], t, MASKED)
    t = jnp.transpose(t, (2, 0, 4, 5, 1, 3, 6))
    return t.reshape(3, NA_HEADS // 2, NA_BAND * GRID_W, 2 * NA_ROWS * GRID_W)


def _gqa_kernel(trips_ref, q_ref, k_ref, v_ref, o_ref, vt_ref, s0_ref, s1_ref, acc_ref, *, seq):
    tq, tk = _gqa_q_tile(seq), GQA_K_TILE
    n_chunks = seq // tk
    unroll = _gqa_unroll(seq)
    s_refs = (s0_ref, s1_ref)
    MAX_ROWS = 8

    @pl.when(pl.program_id(1) == 0)
    def _():
        ones = jnp.ones((HEAD_DIM, tk), jnp.bfloat16)
        for j in range(n_chunks):
            t = v_ref[0, j * tk:(j + 1) * tk, :].astype(jnp.float32).T
            for hk in range(GQA_KV_HEADS):
                vt_ref[hk, j, :HEAD_DIM, :] = t[hk * HEAD_DIM:(hk + 1) * HEAD_DIM].astype(jnp.bfloat16)
                vt_ref[hk, j, HEAD_DIM:, :] = ones

    low = _low_half((tq, LANES))
    heads = [(c, hk) for c in range(GQA_Q_WIDTH // LANES) for hk in range(GQA_KV_HEADS)]

    def masked_q(c, hk):
        qc = q_ref[0, :, c * LANES:(c + 1) * LANES].astype(jnp.float32)
        own = low if hk == 0 else jnp.logical_not(low)
        return jnp.where(own, qc, 0.0).astype(jnp.bfloat16)

    def scores(j, qm, slot, mx):
        r0 = pl.multiple_of(j * tk, tk)
        s = lax.dot_general(k_ref[0, pl.ds(r0, tk), :], qm, (((1,), (1,)), ((), ())),
                            preferred_element_type=jnp.float32)
        s_refs[slot][pl.ds(r0, tk), :] = s
        return jnp.maximum(mx, jnp.max(s.reshape(tk // MAX_ROWS, MAX_ROWS, tq), axis=0))

    def weighted(j, hk, slot, m):
        r0 = pl.multiple_of(j * tk, tk)
        p = jnp.exp2(s_refs[slot][pl.ds(r0, tk), :] - m).astype(jnp.bfloat16)
        acc_ref[...] += jnp.dot(vt_ref[hk, j], p, preferred_element_type=jnp.float32)

    def chunk_loop(step, init):
        def body(t, carry):
            for u in range(unroll):
                carry = step(t * unroll + u, carry)
            return carry
        return lax.fori_loop(0, trips_ref[0], body, init)

    mx0 =jnp.full((MAX_ROWS, tq), -jnp.inf, jnp.float32)
    qm = masked_q(*heads[0])
    mx = chunk_loop(lambda j, mx, qm=qm: scores(j, qm, 0, mx), mx0)
    halves = []
    for i in range(1, len(heads) + 1):
        m = jnp.max(mx, axis=0, keepdims=True)
        c_prev, hk_prev = heads[i - 1]
        cur = (i - 1) % 2
        acc_ref[...] = jnp.zeros_like(acc_ref)
        if i < len(heads):
            qm = masked_q(*heads[i])

            def both(j, mx, qm=qm, m=m, hk_prev=hk_prev, cur=cur):
                weighted(j, hk_prev, cur, m)
                return scores(j, qm, 1 - cur, mx)

            mx = chunk_loop(both, mx0)
        else:
            def last(j, carry, m=m, hk_prev=hk_prev, cur=cur):
                weighted(j, hk_prev, cur, m)
                return carry

            chunk_loop(last, 0)
        acc = acc_ref[...]
        halves.append(acc[:HEAD_DIM] / acc[HEAD_DIM:HEAD_DIM + 1])
        if hk_prev == GQA_KV_HEADS - 1:
            o_ref[0, :, c_prev * LANES:(c_prev + 1) * LANES] = jnp.concatenate(halves, axis=0).T.astype(o_ref.dtype)
            halves = []


def _gqa_unroll(seq):
    return min(GQA_MAX_UNROLL, seq // GQA_K_TILE)


def _gqa_q_tile(seq):
    return min(seq, GQA_SCORE_BYTES // (2 * 4 * seq))


def _gqa_attention(proj3):
    nb, seq, _ = proj3.shape
    tq, tk = _gqa_q_tile(seq), GQA_K_TILE
    unroll = _gqa_unroll(seq)
    assert seq % (tk * unroll) == 0 and seq % tq == 0
    q_blk = 0
    k_blk = GQA_Q_WIDTH // GQA_KV_WIDTH
    trips = jnp.full((1,), seq // (tk * unroll), jnp.int32)
    return pl.pallas_call(
        functools.partial(_gqa_kernel, seq=seq),
        grid=(nb, seq // tq),
        in_specs=[
            pl.BlockSpec(memory_space=pltpu.SMEM),
            pl.BlockSpec((1, tq, GQA_Q_WIDTH), lambda b, i: (b, i, q_blk)),
            pl.BlockSpec((1, seq, GQA_KV_WIDTH), lambda b, i: (b, 0, k_blk)),
            pl.BlockSpec((1, seq, GQA_KV_WIDTH), lambda b, i: (b, 0, k_blk + 1)),
        ],
        out_specs=pl.BlockSpec((1, tq, GQA_Q_WIDTH), lambda b, i: (b, i, 0)),
        out_shape=jax.ShapeDtypeStruct((nb, seq, GQA_Q_WIDTH), jnp.bfloat16),
        scratch_shapes=[
            pltpu.VMEM((GQA_KV_HEADS, seq // tk, LANES, tk), jnp.bfloat16),
            pltpu.VMEM((seq, tq), jnp.float32),
            pltpu.VMEM((seq, tq), jnp.float32),
            pltpu.VMEM((LANES, tq), jnp.float32),
        ],
        compiler_params=pltpu.CompilerParams(dimension_semantics=("parallel", "arbitrary"),
                                             vmem_limit_bytes=VMEM_LIMIT),
        name="gqa_attention",
    )(trips, proj3, proj3, proj3)


def _sigmoid(x):
    return 1.0 / (1.0 + jnp.exp(-x))


def _out_proj_kernel(x_ref, oa_ref, ob_ref, g_ref, scale_ref, shift_ref, gate_ref, wzg_ref, bg_ref,
                     wpa_ref, wpb_ref, wout_ref, fg_ref, o_ref, *, final):
    sub = x_ref.shape[0] // TOKEN_SUBTILES
    for t in range(TOKEN_SUBTILES):
        rows = slice(t * sub, (t + 1) * sub)
        x = x_ref[rows, :]
        hid = _modulated_norm(x, g_ref[...], scale_ref[0], shift_ref[0])
        zg = jnp.dot(hid.astype(jnp.bfloat16), wzg_ref[...], preferred_element_type=jnp.float32)
        za = zg[:, :NA_WIDTH]
        zb = zg[:, NA_WIDTH:NA_WIDTH + GQA_Q_WIDTH]
        ua = oa_ref[rows, :].astype(jnp.float32) * (za * _sigmoid(za))
        ub = ob_ref[rows, :].astype(jnp.float32) * (zb * _sigmoid(zb))
        ya = jnp.dot(ua.astype(jnp.bfloat16), wpa_ref[...], preferred_element_type=jnp.float32)
        yb = jnp.dot(ub.astype(jnp.bfloat16), wpb_ref[...], preferred_element_type=jnp.float32)
        gates = _sigmoid(zg[:, NA_WIDTH + GQA_Q_WIDTH:] + bg_ref[...])
        merged = gates[:, :D_MODEL] * ya + gates[:, D_MODEL:] * yb
        y = jnp.dot(merged.astype(jnp.bfloat16), wout_ref[...], preferred_element_type=jnp.float32)
        out = x + gate_ref[0] * y
        if final:
            ms = jnp.mean(out * out, axis=-1, keepdims=True)
            out = out * lax.rsqrt(ms + EPS) * fg_ref[...]
        o_ref[rows, :] = out


def _out_projection(x2, oa, ob, norm_g, scale, shift, gate, wzg, b_gate, wpa, wpb, wout, final_g, seq, final):
    tokens = x2.shape[0]
    tm = TOKEN_TILE
    per_seq = seq // tm
    vec = lambda i: (0, 0)
    per_batch = lambda i: (i // per_seq, 0, 0)
    return pl.pallas_call(
        functools.partial(_out_proj_kernel, final=final),
        grid=(tokens // tm,),
        in_specs=[
            pl.BlockSpec((tm, D_MODEL), lambda i: (i, 0)),
            pl.BlockSpec((tm, NA_WIDTH), lambda i: (i, 0)),
            pl.BlockSpec((tm, GQA_Q_WIDTH), lambda i: (i, 0)),
            pl.BlockSpec((1, D_MODEL), vec),
            pl.BlockSpec((1, 1, D_MODEL), per_batch),
            pl.BlockSpec((1, 1, D_MODEL), per_batch),
            pl.BlockSpec((1, 1, D_MODEL), per_batch),
            pl.BlockSpec((D_MODEL, ZG_WIDTH), vec),
            pl.BlockSpec((1, 2 * D_MODEL), vec),
            pl.BlockSpec((NA_WIDTH, D_MODEL), vec),
            pl.BlockSpec((GQA_Q_WIDTH, D_MODEL), vec),
            pl.BlockSpec((D_MODEL, D_MODEL), vec),
            pl.BlockSpec((1, D_MODEL), vec),
        ],
        out_specs=pl.BlockSpec((tm, D_MODEL), lambda i: (i, 0)),
        out_shape=jax.ShapeDtypeStruct((tokens, D_MODEL), jnp.float32),
        compiler_params=pltpu.CompilerParams(dimension_semantics=("parallel",), vmem_limit_bytes=VMEM_LIMIT),
        name="out_projection",
    )(x2, oa, ob, norm_g, scale, shift, gate, wzg, b_gate, wpa, wpb, wout, final_g)


def _rope_tables(seq):
    t = jnp.arange(seq)
    row = (t // GRID_W).astype(jnp.float32)
    col = (t % GRID_W).astype(jnp.float32)
    inv = ROPE_THETA ** (-jnp.arange(0, ROPE_AXIS_DIM, 2, dtype=jnp.float32) / ROPE_AXIS_DIM)
    ang_r, ang_c = row[:, None] * inv, col[:, None] * inv
    cos = jnp.concatenate([jnp.cos(ang_r)] * 2 + [jnp.cos(ang_c)] * 2, axis=-1)
    sin = jnp.concatenate([-jnp.sin(ang_r), jnp.sin(ang_r), -jnp.sin(ang_c), jnp.sin(ang_c)], axis=-1)
    return jnp.tile(cos, (1, 2)), jnp.tile(sin, (1, 2))


def _prepare_layer(l, norm_g, w_in, b_gate, rpb, q_norm_g, k_norm_g, w_pa, w_pb, w_out):
    bounds = np.concatenate([[0], np.cumsum(IN_SPLITS)])
    qa, ka, va, za, qb, kb, vb, zb, ga, gb = (w_in[l][:, bounds[i]:bounds[i + 1]] for i in range(10))
    bf = jnp.bfloat16

    def pair_heads(w):
        r = w.shape[0]
        w = w.reshape(r, GQA_KV_HEADS, GQA_GROUP, HEAD_DIM)
        return jnp.transpose(w, (0, 2, 1, 3)).reshape(r, GQA_Q_WIDTH)

    return dict(
        norm_g=norm_g[l][None, :],
        w_gqa=jnp.concatenate([pair_heads(qb), kb, vb], axis=1).astype(bf),
        w_na=jnp.concatenate([qa * (ATTN_SCALE * LOG2_E), ka, va], axis=1).astype(bf),
        wzg=jnp.concatenate([za, pair_heads(zb), ga, gb], axis=1).astype(bf),
        b_gate=b_gate[l][None, :],
        bias=_na_bias_tables(rpb[l]),
        qg=jnp.tile(q_norm_g[l] * (ATTN_SCALE * LOG2_E), 2)[None, :],
        kg=jnp.tile(k_norm_g[l], 2)[None, :],
        wpa=w_pa[l].astype(bf),
        wpb=jnp.transpose(w_pb[l].reshape(GQA_KV_HEADS, GQA_GROUP, HEAD_DIM, D_MODEL),
                          (1, 0, 2, 3)).reshape(GQA_Q_WIDTH, D_MODEL).astype(bf),
        wout=w_out[l].astype(bf),
    )


def _trunk(x, mod, layers, final_g):
    nb, seq, _ = x.shape
    x2 = x.reshape(nb * seq, D_MODEL)
    cos_t, sin_t = _rope_tables(seq)
    for l, p in enumerate(layers):
        shift, scale, gate = (mod[l][:, None, i * D_MODEL:(i + 1) * D_MODEL] for i in range(3))
        proj_gqa, proj_na = _in_projection(x2, p["norm_g"], scale, shift, p["w_gqa"], p["w_na"], cos_t, sin_t,
                                           p["qg"], p["kg"], seq)
        oa = _na_attention(proj_na.reshape(nb, seq, NA_PROJ_WIDTH), p["bias"]).reshape(nb * seq, NA_WIDTH)
        ob = _gqa_attention(proj_gqa.reshape(nb, seq, GQA_PROJ_WIDTH)).reshape(nb * seq, GQA_Q_WIDTH)
        x2 = _out_projection(x2, oa, ob, p["norm_g"], scale, shift, gate, p["wzg"], p["b_gate"],
                             p["wpa"], p["wpb"], p["wout"], final_g[None, :], seq, final=(l == len(layers) - 1))
    return x2.reshape(nb, seq, D_MODEL)


def kernel(x_prompt, x_sample, c_prompt, c_sample, norm_g, w_ada, b_ada, w_in, b_gate, rpb, q_norm_g, k_norm_g, w_pa, w_pb, w_out, final_g):
    layers = [_prepare_layer(l, norm_g, w_in, b_gate, rpb, q_norm_g, k_norm_g, w_pa, w_pb, w_out)
              for l in range(DEPTH)]
    n_prompt = c_prompt.shape[0]
    mod = _modulation(jnp.concatenate([c_prompt, c_sample], axis=0), w_ada, b_ada)
    y_prompt = _trunk(x_prompt, mod[:, :n_prompt], layers, final_g)
    y_sample = _trunk(x_sample, mod[:, n_prompt:], layers, final_g)
    return (y_prompt, y_sample)
```

```python
import functools

import jax
import jax.numpy as jnp
import numpy as np
from jax import lax
from jax.experimental import pallas as pl
from jax.experimental.pallas import tpu as pltpu

D_MODEL = 1024
DEPTH = 4
GRID_W = 64
HEAD_DIM = 64
NA_HEADS = 8
NA_WIDTH = NA_HEADS * HEAD_DIM
NA_WIN_H = 8
NA_WIN_W = 16
GQA_Q_HEADS = 8
GQA_KV_HEADS = 2
GQA_GROUP = GQA_Q_HEADS // GQA_KV_HEADS
GQA_Q_WIDTH = GQA_Q_HEADS * HEAD_DIM
GQA_KV_WIDTH = GQA_KV_HEADS * HEAD_DIM
ROPE_AXIS_DIM = HEAD_DIM // 2
ROPE_THETA = 10000.0
EPS = 1e-6
ATTN_SCALE = HEAD_DIM ** -0.5
IN_SPLITS = (NA_WIDTH,) * 4 + (GQA_Q_WIDTH, GQA_KV_WIDTH, GQA_KV_WIDTH, GQA_Q_WIDTH, D_MODEL, D_MODEL)
MASKED = -1e30
LOG2_E = 1.4426950408889634

LANES = 128
NA_PROJ_WIDTH = 3 * NA_WIDTH
GQA_PROJ_WIDTH = GQA_Q_WIDTH + 2 * GQA_KV_WIDTH
ZG_WIDTH = NA_WIDTH + GQA_Q_WIDTH + 2 * D_MODEL
TOKEN_TILE = 512
TOKEN_SUBTILES = 2
NA_ROWS = 4
NA_BAND = 12
NA_KEY_TILE = NA_ROWS * GRID_W
NA_SUBSTEPS = 4
GQA_SCORE_BYTES = 16 * 1024 * 1024
GQA_K_TILE = 512
GQA_MAX_UNROLL = 8
VMEM_LIMIT = 56 * 1024 * 1024


def _low_half(shape):
    return lax.broadcasted_iota(jnp.int32, shape, len(shape) - 1) < HEAD_DIM


def _mod_kernel(c_ref, w_ref, b_ref, o_ref):
    c = c_ref[...]
    sc = c * (1.0 / (1.0 + jnp.exp(-c)))
    o_ref[0] = jnp.dot(sc, w_ref[0], precision=lax.Precision.HIGHEST,
                       preferred_element_type=jnp.float32) + b_ref[0]


def _modulation(c_all, w_ada, b_ada):
    nb = c_all.shape[0]
    n_chunks = 3
    return pl.pallas_call(
        _mod_kernel,
        grid=(DEPTH, n_chunks),
        in_specs=[
            pl.BlockSpec((nb, D_MODEL), lambda l, j: (0, 0)),
            pl.BlockSpec((1, D_MODEL, D_MODEL), lambda l, j: (l, 0, j)),
            pl.BlockSpec((1, 1, D_MODEL), lambda l, j: (l, 0, j)),
        ],
        out_specs=pl.BlockSpec((1, nb, D_MODEL), lambda l, j: (l, 0, j)),
        out_shape=jax.ShapeDtypeStruct((DEPTH, nb, 3 * D_MODEL), jnp.float32),
        compiler_params=pltpu.CompilerParams(vmem_limit_bytes=VMEM_LIMIT),
        name="adaln_modulation",
    )(c_all, w_ada, b_ada.reshape(DEPTH, 1, 3 * D_MODEL))


def _modulated_norm(x, g, scale, shift):
    ms = jnp.mean(x * x, axis=-1, keepdims=True)
    y = x * lax.rsqrt(ms + EPS) * g
    return y * (1.0 + scale) + shift


def _head_norm_rope(x, g, cos, sin_signed, low):
    sq = x * x
    lo = jnp.sum(jnp.where(low, sq, 0.0), axis=-1, keepdims=True)
    hi = jnp.sum(jnp.where(low, 0.0, sq), axis=-1, keepdims=True)
    ms = jnp.where(low, lo, hi) * (1.0 / HEAD_DIM)
    y = x * lax.rsqrt(ms + EPS) * g
    lane = lax.broadcasted_iota(jnp.int32, x.shape, 1)
    quarter = ROPE_AXIS_DIM // 2
    partner = jnp.where((lane & quarter) != 0, pltpu.roll(y, quarter, 1), pltpu.roll(y, LANES - quarter, 1))
    return y * cos + partner * sin_signed


def _in_proj_kernel(x_ref, g_ref, scale_ref, shift_ref, wb_ref, wa_ref, cos_ref, sin_ref, qg_ref, kg_ref,
                    ob_ref, oa_ref):
    n_q_chunks = GQA_Q_WIDTH // LANES
    v0 = GQA_Q_WIDTH + GQA_KV_WIDTH
    sub = x_ref.shape[0] // TOKEN_SUBTILES
    low = _low_half((sub, LANES))
    for t in range(TOKEN_SUBTILES):
        rows = slice(t * sub, (t + 1) * sub)
        hid = _modulated_norm(x_ref[rows, :], g_ref[...], scale_ref[0], shift_ref[0]).astype(jnp.bfloat16)
        acc = jnp.dot(hid, wb_ref[...], preferred_element_type=jnp.float32)
        cos = cos_ref[rows, :]
        sin = sin_ref[rows, :]
        for j in range(n_q_chunks + 1):
            c0 = j * LANES
            gain = qg_ref[...] if j < n_q_chunks else kg_ref[...]
            ob_ref[rows, c0:c0 + LANES] = _head_norm_rope(acc[:, c0:c0 + LANES], gain, cos, sin,
                                                          low).astype(ob_ref.dtype)
        ob_ref[rows, v0:] = acc[:, v0:].astype(ob_ref.dtype)
        oa_ref[rows, :] = jnp.dot(hid, wa_ref[...], preferred_element_type=jnp.float32).astype(oa_ref.dtype)


def _in_projection(x2, norm_g, scale, shift, wb, wa, cos_t, sin_t, qg, kg, seq):
    tokens = x2.shape[0]
    tm = TOKEN_TILE
    per_seq = seq // tm
    vec = lambda i: (0, 0)
    return pl.pallas_call(
        _in_proj_kernel,
        grid=(tokens // tm,),
        in_specs=[
            pl.BlockSpec((tm, D_MODEL), lambda i: (i, 0)),
            pl.BlockSpec((1, D_MODEL), vec),
            pl.BlockSpec((1, 1, D_MODEL), lambda i: (i // per_seq, 0, 0)),
            pl.BlockSpec((1, 1, D_MODEL), lambda i: (i // per_seq, 0, 0)),
            pl.BlockSpec((D_MODEL, GQA_PROJ_WIDTH), vec),
            pl.BlockSpec((D_MODEL, NA_PROJ_WIDTH), vec),
            pl.BlockSpec((tm, LANES), lambda i: (i % per_seq, 0)),
            pl.BlockSpec((tm, LANES), lambda i: (i % per_seq, 0)),
            pl.BlockSpec((1, LANES), vec),
            pl.BlockSpec((1, LANES), vec),
        ],
        out_specs=[pl.BlockSpec((tm, GQA_PROJ_WIDTH), lambda i: (i, 0)),
                   pl.BlockSpec((tm, NA_PROJ_WIDTH), lambda i: (i, 0))],
        out_shape=[jax.ShapeDtypeStruct((tokens, GQA_PROJ_WIDTH), jnp.bfloat16),
                   jax.ShapeDtypeStruct((tokens, NA_PROJ_WIDTH), jnp.bfloat16)],
        compiler_params=pltpu.CompilerParams(dimension_semantics=("parallel",), vmem_limit_bytes=VMEM_LIMIT),
        name="in_projection",
    )(x2, norm_g, scale, shift, wb, wa, cos_t, sin_t, qg, kg)


def _na_kernel(q_ref, k_ref, v_ref, bias_ref, o_ref, vt_ref, *, grid_rows):
    step = pl.program_id(1)
    seq = grid_rows * GRID_W
    nk = NA_BAND * GRID_W
    nq = NA_ROWS * GRID_W
    kt = NA_KEY_TILE
    last_group = grid_rows // NA_ROWS - 1

    @pl.when(step == 0)
    def _():
        row = lax.broadcasted_iota(jnp.int32, (NA_WIDTH, kt), 0)
        even_rows = (row & HEAD_DIM) == 0
        for j in range(seq // kt):
            t = v_ref[0, j * kt:(j + 1) * kt, :].astype(jnp.float32).T
            vt_ref[0, j] = jnp.where(even_rows, t, 1.0).astype(jnp.bfloat16)
            vt_ref[1, j] = jnp.where(even_rows, 1.0, t).astype(jnp.bfloat16)

    low = _low_half((nq, LANES))
    groups = [NA_SUBSTEPS * step + sub for sub in range(NA_SUBSTEPS)]
    band_rows = [jnp.clip(NA_ROWS * g - NA_WIN_H // 2, 0, grid_rows - NA_BAND) for g in groups]
    variants = [(g > 0).astype(jnp.int32) + (g == last_group).astype(jnp.int32) for g in groups]

    def scores(sub, c):
        k0 = pl.multiple_of(band_rows[sub] * GRID_W, kt)
        qc = q_ref[0, sub * nq:(sub + 1) * nq, c * LANES:(c + 1) * LANES].astype(jnp.float32)
        qm = jnp.concatenate([jnp.where(low, qc, 0.0), jnp.where(low, 0.0, qc)], axis=0).astype(jnp.bfloat16)
        s = lax.dot_general(k_ref[0, pl.ds(k0, nk), c * LANES:(c + 1) * LANES], qm, (((1,), (1,)), ((), ())),
                            preferred_element_type=jnp.float32)
        s = s + bias_ref[variants[sub], c]
        return s, jnp.max(jnp.max(s.reshape(nk // 8, 8, 2 * nq), axis=0), axis=0, keepdims=True)

    def weighted(sub, c, s, m):
        kb = band_rows[sub] // (kt // GRID_W)
        accs = [jnp.zeros((LANES, nq), jnp.float32) for _ in range(2)]
        for t in range(nk // kt):
            p = jnp.exp2(s[t * kt:(t + 1) * kt] - m).astype(jnp.bfloat16)
            for half in range(2):
                accs[half] = accs[half] + jnp.dot(vt_ref[half, kb + t, c * LANES:(c + 1) * LANES, :],
                                                  p[:, half * nq:(half + 1) * nq],
                                                  preferred_element_type=jnp.float32)
        halves = []
        for half in range(2):
            other = (1 - half) * HEAD_DIM
            halves.append(accs[half][half * HEAD_DIM:(half + 1) * HEAD_DIM] / accs[half][other:other + 1])
        o_ref[0, sub * nq:(sub + 1) * nq, c * LANES:(c + 1) * LANES] = (
            jnp.concatenate(halves, axis=0).T.astype(o_ref.dtype))

    units = [(sub, c) for c in range(NA_WIDTH // LANES) for sub in range(NA_SUBSTEPS)]
    ahead = NA_SUBSTEPS
    pending = [scores(*u) for u in units[:ahead]]
    for n, u in enumerate(units):
        if n + ahead < len(units):
            pending.append(scores(*units[n + ahead]))
        weighted(*u, *pending.pop(0))


def _na_attention(proj3, bias):
    nb, seq, _ = proj3.shape
    grid_rows = seq // GRID_W
    assert grid_rows % (NA_ROWS * NA_SUBSTEPS) == 0 and grid_rows >= NA_BAND
    steps = grid_rows // (NA_ROWS * NA_SUBSTEPS)
    nq = NA_ROWS * GRID_W
    return pl.pallas_call(
        functools.partial(_na_kernel, grid_rows=grid_rows),
        grid=(nb, steps),
        in_specs=[
            pl.BlockSpec((1, NA_SUBSTEPS * nq, NA_WIDTH), lambda b, g: (b, g, 0)),
            pl.BlockSpec((1, seq, NA_WIDTH), lambda b, g: (b, 0, 1)),
            pl.BlockSpec((1, seq, NA_WIDTH), lambda b, g: (b, 0, 2)),
            pl.BlockSpec(bias.shape, lambda b, g: (0, 0, 0, 0), pipeline_mode=pl.Buffered(1)),
        ],
        out_specs=pl.BlockSpec((1, NA_SUBSTEPS * nq, NA_WIDTH), lambda b, g: (b, g, 0)),
        out_shape=jax.ShapeDtypeStruct((nb, seq, NA_WIDTH), jnp.bfloat16),
        scratch_shapes=[pltpu.VMEM((2, seq // NA_KEY_TILE, NA_WIDTH, NA_KEY_TILE), jnp.bfloat16)],
        compiler_params=pltpu.CompilerParams(dimension_semantics=("parallel", "arbitrary"),
                                             vmem_limit_bytes=VMEM_LIMIT),
        name="neighbourhood_attention",
    )(proj3, proj3, proj3, bias)


def _na_bias_tables(rpb_l):
    i = np.arange(NA_ROWS)[:, None]
    a = np.arange(NA_BAND)[None, :]
    d = a - i
    start = np.stack([np.zeros_like(d), np.zeros_like(d) + i, np.zeros_like(d) + 4])
    row_ok = (a[None] >= start) & (a[None] < start + NA_WIN_H)
    row_off = np.stack([d + 7, d + 3, d - 1])
    row_off = np.clip(row_off, 0, 2 * NA_WIN_H - 2)
    cq = np.arange(GRID_W)[:, None]
    ck = np.arange(GRID_W)[None, :]
    c0 = np.clip(cq - NA_WIN_W // 2, 0, GRID_W - NA_WIN_W)
    col_ok = (ck >= c0) & (ck < c0 + NA_WIN_W)
    col_off = np.clip(ck - cq + NA_WIN_W - 1, 0, 2 * NA_WIN_W - 2)
    cols = jnp.where(col_ok.T[None, None], rpb_l.astype(jnp.float32)[:, :, col_off.T] * LOG2_E, MASKED)
    t = jnp.stack([cols[:, r] for r in row_off.reshape(-1)], axis=1)
    t = t.reshape(NA_HEADS // 2, 2, 3, NA_ROWS, NA_BAND, GRID_W, GRID_W)
    t = jnp.where(row_ok[None, None, :, :, :, None, None], t, MASKED)
    t = jnp.transpose(t, (2, 0, 4, 5, 1, 3, 6))
    return t.reshape(3, NA_HEADS // 2, NA_BAND * GRID_W, 2 * NA_ROWS * GRID_W)


def _gqa_kernel(trips_ref, q_ref, k_ref, v_ref, o_ref, vt_ref, s0_ref, s1_ref, acc_ref, *, seq):
    tq, tk = _gqa_q_tile(seq), GQA_K_TILE
    n_chunks = seq // tk
    unroll = _gqa_unroll(seq)
    s_refs = (s0_ref, s1_ref)
    MAX_ROWS = 8

    @pl.when(pl.program_id(1) == 0)
    def _():
        ones = jnp.ones((HEAD_DIM, tk), jnp.bfloat16)
        for j in range(n_chunks):
            t = v_ref[0, j * tk:(j + 1) * tk, :].astype(jnp.float32).T
            for hk in range(GQA_KV_HEADS):
                vt_ref[hk, j, :HEAD_DIM, :] = t[hk * HEAD_DIM:(hk + 1) * HEAD_DIM].astype(jnp.bfloat16)
                vt_ref[hk, j, HEAD_DIM:, :] = ones

    low = _low_half((tq, LANES))
    heads = [(c, hk) for c in range(GQA_Q_WIDTH // LANES) for hk in range(GQA_KV_HEADS)]

    def masked_q(c, hk):
        qc = q_ref[0, :, c * LANES:(c + 1) * LANES].astype(jnp.float32)
        own = low if hk == 0 else jnp.logical_not(low)
        return jnp.where(own, qc, 0.0).astype(jnp.bfloat16)

    def scores(j, qm, slot, mx):
        r0 = pl.multiple_of(j * tk, tk)
        s = lax.dot_general(k_ref[0, pl.ds(r0, tk), :], qm, (((1,), (1,)), ((), ())),
                            preferred_element_type=jnp.float32)
        s_refs[slot][pl.ds(r0, tk), :] = s
        return jnp.maximum(mx, jnp.max(s.reshape(tk // MAX_ROWS, MAX_ROWS, tq), axis=0))

    def weighted(j, hk, slot, m):
        r0 = pl.multiple_of(j * tk, tk)
        p = jnp.exp2(s_refs[slot][pl.ds(r0, tk), :] - m).astype(jnp.bfloat16)
        acc_ref[...] += jnp.dot(vt_ref[hk, j], p, preferred_element_type=jnp.float32)

    def chunk_loop(step, init):
        def body(t, carry):
            for u in range(unroll):
                carry = step(t * unroll + u, carry)
            return carry
        return lax.fori_loop(0, trips_ref[0], body, init)

    mx0 =jnp.full((MAX_ROWS, tq), -jnp.inf, jnp.float32)
    qm = masked_q(*heads[0])
    mx = chunk_loop(lambda j, mx, qm=qm: scores(j, qm, 0, mx), mx0)
    halves = []
    for i in range(1, len(heads) + 1):
        m = jnp.max(mx, axis=0, keepdims=True)
        c_prev, hk_prev = heads[i - 1]
        cur = (i - 1) % 2
        acc_ref[...] = jnp.zeros_like(acc_ref)
        if i < len(heads):
            qm = masked_q(*heads[i])

            def both(j, mx, qm=qm, m=m, hk_prev=hk_prev, cur=cur):
                weighted(j, hk_prev, cur, m)
                return scores(j, qm, 1 - cur, mx)

            mx = chunk_loop(both, mx0)
        else:
            def last(j, carry, m=m, hk_prev=hk_prev, cur=cur):
                weighted(j, hk_prev, cur, m)
                return carry

            chunk_loop(last, 0)
        acc = acc_ref[...]
        halves.append(acc[:HEAD_DIM] / acc[HEAD_DIM:HEAD_DIM + 1])
        if hk_prev == GQA_KV_HEADS - 1:
            o_ref[0, :, c_prev * LANES:(c_prev + 1) * LANES] = jnp.concatenate(halves, axis=0).T.astype(o_ref.dtype)
            halves = []


def _gqa_unroll(seq):
    return min(GQA_MAX_UNROLL, seq // GQA_K_TILE)


def _gqa_q_tile(seq):
    return min(seq, GQA_SCORE_BYTES // (2 * 4 * seq))


def _gqa_attention(proj3):
    nb, seq, _ = proj3.shape
    tq, tk = _gqa_q_tile(seq), GQA_K_TILE
    unroll = _gqa_unroll(seq)
    assert seq % (tk * unroll) == 0 and seq % tq == 0
    q_blk = 0
    k_blk = GQA_Q_WIDTH // GQA_KV_WIDTH
    trips = jnp.full((1,), seq // (tk * unroll), jnp.int32)
    return pl.pallas_call(
        functools.partial(_gqa_kernel, seq=seq),
        grid=(nb, seq // tq),
        in_specs=[
            pl.BlockSpec(memory_space=pltpu.SMEM),
            pl.BlockSpec((1, tq, GQA_Q_WIDTH), lambda b, i: (b, i, q_blk)),
            pl.BlockSpec((1, seq, GQA_KV_WIDTH), lambda b, i: (b, 0, k_blk)),
            pl.BlockSpec((1, seq, GQA_KV_WIDTH), lambda b, i: (b, 0, k_blk + 1)),
        ],
        out_specs=pl.BlockSpec((1, tq, GQA_Q_WIDTH), lambda b, i: (b, i, 0)),
        out_shape=jax.ShapeDtypeStruct((nb, seq, GQA_Q_WIDTH), jnp.bfloat16),
        scratch_shapes=[
            pltpu.VMEM((GQA_KV_HEADS, seq // tk, LANES, tk), jnp.bfloat16),
            pltpu.VMEM((seq, tq), jnp.float32),
            pltpu.VMEM((seq, tq), jnp.float32),
            pltpu.VMEM((LANES, tq), jnp.float32),
        ],
        compiler_params=pltpu.CompilerParams(dimension_semantics=("parallel", "arbitrary"),
                                             vmem_limit_bytes=VMEM_LIMIT),
        name="gqa_attention",
    )(trips, proj3, proj3, proj3)


def _sigmoid(x):
    return 1.0 / (1.0 + jnp.exp(-x))


def _out_proj_kernel(x_ref, oa_ref, ob_ref, g_ref, scale_ref, shift_ref, gate_ref, wzg_ref, bg_ref,
                     wpa_ref, wpb_ref, wout_ref, fg_ref, o_ref, *, final):
    sub = x_ref.shape[0] // TOKEN_SUBTILES
    for t in range(TOKEN_SUBTILES):
        rows = slice(t * sub, (t + 1) * sub)
        x = x_ref[rows, :]
        hid = _modulated_norm(x, g_ref[...], scale_ref[0], shift_ref[0])
        zg = jnp.dot(hid.astype(jnp.bfloat16), wzg_ref[...], preferred_element_type=jnp.float32)
        za = zg[:, :NA_WIDTH]
        zb = zg[:, NA_WIDTH:NA_WIDTH + GQA_Q_WIDTH]
        ua = oa_ref[rows, :].astype(jnp.float32) * (za * _sigmoid(za))
        ub = ob_ref[rows, :].astype(jnp.float32) * (zb * _sigmoid(zb))
        ya = jnp.dot(ua.astype(jnp.bfloat16), wpa_ref[...], preferred_element_type=jnp.float32)
        yb = jnp.dot(ub.astype(jnp.bfloat16), wpb_ref[...], preferred_element_type=jnp.float32)
        gates = _sigmoid(zg[:, NA_WIDTH + GQA_Q_WIDTH:] + bg_ref[...])
        merged = gates[:, :D_MODEL] * ya + gates[:, D_MODEL:] * yb
        y = jnp.dot(merged.astype(jnp.bfloat16), wout_ref[...], preferred_element_type=jnp.float32)
        out = x + gate_ref[0] * y
        if final:
            ms = jnp.mean(out * out, axis=-1, keepdims=True)
            out = out * lax.rsqrt(ms + EPS) * fg_ref[...]
        o_ref[rows, :] = out


def _out_projection(x2, oa, ob, norm_g, scale, shift, gate, wzg, b_gate, wpa, wpb, wout, final_g, seq, final):
    tokens = x2.shape[0]
    tm = TOKEN_TILE
    per_seq = seq // tm
    vec = lambda i: (0, 0)
    per_batch = lambda i: (i // per_seq, 0, 0)
    return pl.pallas_call(
        functools.partial(_out_proj_kernel, final=final),
        grid=(tokens // tm,),
        in_specs=[
            pl.BlockSpec((tm, D_MODEL), lambda i: (i, 0)),
            pl.BlockSpec((tm, NA_WIDTH), lambda i: (i, 0)),
            pl.BlockSpec((tm, GQA_Q_WIDTH), lambda i: (i, 0)),
            pl.BlockSpec((1, D_MODEL), vec),
            pl.BlockSpec((1, 1, D_MODEL), per_batch),
            pl.BlockSpec((1, 1, D_MODEL), per_batch),
            pl.BlockSpec((1, 1, D_MODEL), per_batch),
            pl.BlockSpec((D_MODEL, ZG_WIDTH), vec),
            pl.BlockSpec((1, 2 * D_MODEL), vec),
            pl.BlockSpec((NA_WIDTH, D_MODEL), vec),
            pl.BlockSpec((GQA_Q_WIDTH, D_MODEL), vec),
            pl.BlockSpec((D_MODEL, D_MODEL), vec),
            pl.BlockSpec((1, D_MODEL), vec),
        ],
        out_specs=pl.BlockSpec((tm, D_MODEL), lambda i: (i, 0)),
        out_shape=jax.ShapeDtypeStruct((tokens, D_MODEL), jnp.float32),
        compiler_params=pltpu.CompilerParams(dimension_semantics=("parallel",), vmem_limit_bytes=VMEM_LIMIT),
        name="out_projection",
    )(x2, oa, ob, norm_g, scale, shift, gate, wzg, b_gate, wpa, wpb, wout, final_g)


def _rope_tables(seq):
    t = jnp.arange(seq)
    row = (t // GRID_W).astype(jnp.float32)
    col = (t % GRID_W).astype(jnp.float32)
    inv = ROPE_THETA ** (-jnp.arange(0, ROPE_AXIS_DIM, 2, dtype=jnp.float32) / ROPE_AXIS_DIM)
    ang_r, ang_c = row[:, None] * inv, col[:, None] * inv
    cos = jnp.concatenate([jnp.cos(ang_r)] * 2 + [jnp.cos(ang_c)] * 2, axis=-1)
    sin = jnp.concatenate([-jnp.sin(ang_r), jnp.sin(ang_r), -jnp.sin(ang_c), jnp.sin(ang_c)], axis=-1)
    return jnp.tile(cos, (1, 2)), jnp.tile(sin, (1, 2))


def _prepare_layer(l, norm_g, w_in, b_gate, rpb, q_norm_g, k_norm_g, w_pa, w_pb, w_out):
    bounds = np.concatenate([[0], np.cumsum(IN_SPLITS)])
    qa, ka, va, za, qb, kb, vb, zb, ga, gb = (w_in[l][:, bounds[i]:bounds[i + 1]] for i in range(10))
    bf = jnp.bfloat16

    def pair_heads(w):
        r = w.shape[0]
        w = w.reshape(r, GQA_KV_HEADS, GQA_GROUP, HEAD_DIM)
        return jnp.transpose(w, (0, 2, 1, 3)).reshape(r, GQA_Q_WIDTH)

    return dict(
        norm_g=norm_g[l][None, :],
        w_gqa=jnp.concatenate([pair_heads(qb), kb, vb], axis=1).astype(bf),
        w_na=jnp.concatenate([qa * (ATTN_SCALE * LOG2_E), ka, va], axis=1).astype(bf),
        wzg=jnp.concatenate([za, pair_heads(zb), ga, gb], axis=1).astype(bf),
        b_gate=b_gate[l][None, :],
        bias=_na_bias_tables(rpb[l]),
        qg=jnp.tile(q_norm_g[l] * (ATTN_SCALE * LOG2_E), 2)[None, :],
        kg=jnp.tile(k_norm_g[l], 2)[None, :],
        wpa=w_pa[l].astype(bf),
        wpb=jnp.transpose(w_pb[l].reshape(GQA_KV_HEADS, GQA_GROUP, HEAD_DIM, D_MODEL),
                          (1, 0, 2, 3)).reshape(GQA_Q_WIDTH, D_MODEL).astype(bf),
        wout=w_out[l].astype(bf),
    )


def _trunk(x, mod, layers, final_g):
    nb, seq, _ = x.shape
    x2 = x.reshape(nb * seq, D_MODEL)
    cos_t, sin_t = _rope_tables(seq)
    for l, p in enumerate(layers):
        shift, scale, gate = (mod[l][:, None, i * D_MODEL:(i + 1) * D_MODEL] for i in range(3))
        proj_gqa, proj_na = _in_projection(x2, p["norm_g"], scale, shift, p["w_gqa"], p["w_na"], cos_t, sin_t,
                                           p["qg"], p["kg"], seq)
        oa = _na_attention(proj_na.reshape(nb, seq, NA_PROJ_WIDTH), p["bias"]).reshape(nb * seq, NA_WIDTH)
        ob = _gqa_attention(proj_gqa.reshape(nb, seq, GQA_PROJ_WIDTH)).reshape(nb * seq, GQA_Q_WIDTH)
        x2 = _out_projection(x2, oa, ob, p["norm_g"], scale, shift, gate, p["wzg"], p["b_gate"],
                             p["wpa"], p["wpb"], p["wout"], final_g[None, :], seq, final=(l == len(layers) - 1))
    return x2.reshape(nb, seq, D_MODEL)


def kernel(x_prompt, x_sample, c_prompt, c_sample, norm_g, w_ada, b_ada, w_in, b_gate, rpb, q_norm_g, k_norm_g, w_pa, w_pb, w_out, final_g):
    layers = [_prepare_layer(l, norm_g, w_in, b_gate, rpb, q_norm_g, k_norm_g, w_pa, w_pb, w_out)
              for l in range(DEPTH)]
    n_prompt = c_prompt.shape[0]
    mod = _modulation(jnp.concatenate([c_prompt, c_sample], axis=0), w_ada, b_ada)
    y_prompt = _trunk(x_prompt, mod[:, :n_prompt], layers, final_g)
    y_sample = _trunk(x_sample, mod[:, n_prompt:], layers, final_g)
    return (y_prompt, y_sample)
```

```python
import functools

import jax
import jax.numpy as jnp
import numpy as np
from jax import lax
from jax.experimental import pallas as pl
from jax.experimental.pallas import tpu as pltpu

D_MODEL = 1024
DEPTH = 4
GRID_W = 64
HEAD_DIM = 64
NA_HEADS = 8
NA_WIDTH = NA_HEADS * HEAD_DIM
NA_WIN_H = 8
NA_WIN_W = 16
GQA_Q_HEADS = 8
GQA_KV_HEADS = 2
GQA_GROUP = GQA_Q_HEADS // GQA_KV_HEADS
GQA_Q_WIDTH = GQA_Q_HEADS * HEAD_DIM
GQA_KV_WIDTH = GQA_KV_HEADS * HEAD_DIM
ROPE_AXIS_DIM = HEAD_DIM // 2
ROPE_THETA = 10000.0
EPS = 1e-6
ATTN_SCALE = HEAD_DIM ** -0.5
IN_SPLITS = (NA_WIDTH,) * 4 + (GQA_Q_WIDTH, GQA_KV_WIDTH, GQA_KV_WIDTH, GQA_Q_WIDTH, D_MODEL, D_MODEL)
MASKED = -1e30
LOG2_E = 1.4426950408889634

LANES = 128
NA_PROJ_WIDTH = 3 * NA_WIDTH
GQA_PROJ_WIDTH = GQA_Q_WIDTH + 2 * GQA_KV_WIDTH
ZG_WIDTH = NA_WIDTH + GQA_Q_WIDTH + 2 * D_MODEL
TOKEN_TILE = 1024
TOKEN_SUBTILES = 4
NA_ROWS = 4
NA_BAND = 12
NA_KEY_TILE = NA_ROWS * GRID_W
NA_SUBSTEPS = 4
GQA_SCORE_BYTES = 16 * 1024 * 1024
GQA_K_TILE = 512
GQA_MAX_UNROLL = 8
VMEM_LIMIT = 56 * 1024 * 1024


def _low_half(shape):
    return lax.broadcasted_iota(jnp.int32, shape, len(shape) - 1) < HEAD_DIM


def _mod_kernel(c_ref, w_ref, b_ref, o_ref):
    c = c_ref[...]
    sc = c * (1.0 / (1.0 + jnp.exp(-c)))
    o_ref[0] = jnp.dot(sc, w_ref[0], precision=lax.Precision.HIGHEST,
                       preferred_element_type=jnp.float32) + b_ref[0]


def _modulation(c_all, w_ada, b_ada):
    nb = c_all.shape[0]
    n_chunks = 3
    return pl.pallas_call(
        _mod_kernel,
        grid=(DEPTH, n_chunks),
        in_specs=[
            pl.BlockSpec((nb, D_MODEL), lambda l, j: (0, 0)),
            pl.BlockSpec((1, D_MODEL, D_MODEL), lambda l, j: (l, 0, j)),
            pl.BlockSpec((1, 1, D_MODEL), lambda l, j: (l, 0, j)),
        ],
        out_specs=pl.BlockSpec((1, nb, D_MODEL), lambda l, j: (l, 0, j)),
        out_shape=jax.ShapeDtypeStruct((DEPTH, nb, 3 * D_MODEL), jnp.float32),
        compiler_params=pltpu.CompilerParams(vmem_limit_bytes=VMEM_LIMIT),
        name="adaln_modulation",
    )(c_all, w_ada, b_ada.reshape(DEPTH, 1, 3 * D_MODEL))


def _modulated_norm(x, g, scale, shift):
    ms = jnp.mean(x * x, axis=-1, keepdims=True)
    y = x * lax.rsqrt(ms + EPS) * g
    return y * (1.0 + scale) + shift


def _head_norm_rope(x, g, cos, sin_signed, low):
    sq = x * x
    lo = jnp.sum(jnp.where(low, sq, 0.0), axis=-1, keepdims=True)
    hi = jnp.sum(jnp.where(low, 0.0, sq), axis=-1, keepdims=True)
    ms = jnp.where(low, lo, hi) * (1.0 / HEAD_DIM)
    y = x * lax.rsqrt(ms + EPS) * g
    lane = lax.broadcasted_iota(jnp.int32, x.shape, 1)
    quarter = ROPE_AXIS_DIM // 2
    partner = jnp.where((lane & quarter) != 0, pltpu.roll(y, quarter, 1), pltpu.roll(y, LANES - quarter, 1))
    return y * cos + partner * sin_signed


def _in_proj_kernel(x_ref, g_ref, scale_ref, shift_ref, wb_ref, wa_ref, cos_ref, sin_ref, qg_ref, kg_ref,
                    ob_ref, oa_ref):
    n_q_chunks = GQA_Q_WIDTH // LANES
    v0 = GQA_Q_WIDTH + GQA_KV_WIDTH
    sub = x_ref.shape[0] // TOKEN_SUBTILES
    low = _low_half((sub, LANES))
    for t in range(TOKEN_SUBTILES):
        rows = slice(t * sub, (t + 1) * sub)
        hid = _modulated_norm(x_ref[rows, :], g_ref[...], scale_ref[0], shift_ref[0]).astype(jnp.bfloat16)
        acc = jnp.dot(hid, wb_ref[...], preferred_element_type=jnp.float32)
        cos = cos_ref[rows, :]
        sin = sin_ref[rows, :]
        for j in range(n_q_chunks + 1):
            c0 = j * LANES
            gain = qg_ref[...] if j < n_q_chunks else kg_ref[...]
            ob_ref[rows, c0:c0 + LANES] = _head_norm_rope(acc[:, c0:c0 + LANES], gain, cos, sin,
                                                          low).astype(ob_ref.dtype)
        ob_ref[rows, v0:] = acc[:, v0:].astype(ob_ref.dtype)
        oa_ref[rows, :] = jnp.dot(hid, wa_ref[...], preferred_element_type=jnp.float32).astype(oa_ref.dtype)


def _in_projection(x2, norm_g, scale, shift, wb, wa, cos_t, sin_t, qg, kg, seq):
    tokens = x2.shape[0]
    tm = TOKEN_TILE
    per_seq = seq // tm
    vec = lambda i: (0, 0)
    return pl.pallas_call(
        _in_proj_kernel,
        grid=(tokens // tm,),
        in_specs=[
            pl.BlockSpec((tm, D_MODEL), lambda i: (i, 0)),
            pl.BlockSpec((1, D_MODEL), vec),
            pl.BlockSpec((1, 1, D_MODEL), lambda i: (i // per_seq, 0, 0)),
            pl.BlockSpec((1, 1, D_MODEL), lambda i: (i // per_seq, 0, 0)),
            pl.BlockSpec((D_MODEL, GQA_PROJ_WIDTH), vec, pipeline_mode=pl.Buffered(1)),
            pl.BlockSpec((D_MODEL, NA_PROJ_WIDTH), vec, pipeline_mode=pl.Buffered(1)),
            pl.BlockSpec((tm, LANES), lambda i: (i % per_seq, 0)),
            pl.BlockSpec((tm, LANES), lambda i: (i % per_seq, 0)),
            pl.BlockSpec((1, LANES), vec),
            pl.BlockSpec((1, LANES), vec),
        ],
        out_specs=[pl.BlockSpec((tm, GQA_PROJ_WIDTH), lambda i: (i, 0)),
                   pl.BlockSpec((tm, NA_PROJ_WIDTH), lambda i: (i, 0))],
        out_shape=[jax.ShapeDtypeStruct((tokens, GQA_PROJ_WIDTH), jnp.bfloat16),
                   jax.ShapeDtypeStruct((tokens, NA_PROJ_WIDTH), jnp.bfloat16)],
        compiler_params=pltpu.CompilerParams(dimension_semantics=("parallel",), vmem_limit_bytes=VMEM_LIMIT),
        name="in_projection",
    )(x2, norm_g, scale, shift, wb, wa, cos_t, sin_t, qg, kg)


def _na_kernel(q_ref, k_ref, v_ref, bias_ref, o_ref, vt_ref, *, grid_rows):
    step = pl.program_id(1)
    seq = grid_rows * GRID_W
    nk = NA_BAND * GRID_W
    nq = NA_ROWS * GRID_W
    kt = NA_KEY_TILE
    last_group = grid_rows // NA_ROWS - 1

    @pl.when(step == 0)
    def _():
        row = lax.broadcasted_iota(jnp.int32, (NA_WIDTH, kt), 0)
        even_rows = (row & HEAD_DIM) == 0
        for j in range(seq // kt):
            t = v_ref[0, j * kt:(j + 1) * kt, :].astype(jnp.float32).T
            vt_ref[0, j] = jnp.where(even_rows, t, 1.0).astype(jnp.bfloat16)
            vt_ref[1, j] = jnp.where(even_rows, 1.0, t).astype(jnp.bfloat16)

    low = _low_half((nq, LANES))
    groups = [NA_SUBSTEPS * step + sub for sub in range(NA_SUBSTEPS)]
    band_rows = [jnp.clip(NA_ROWS * g - NA_WIN_H // 2, 0, grid_rows - NA_BAND) for g in groups]
    variants = [(g > 0).astype(jnp.int32) + (g == last_group).astype(jnp.int32) for g in groups]

    def scores(sub, c):
        k0 = pl.multiple_of(band_rows[sub] * GRID_W, kt)
        qc = q_ref[0, sub * nq:(sub + 1) * nq, c * LANES:(c + 1) * LANES].astype(jnp.float32)
        qm = jnp.concatenate([jnp.where(low, qc, 0.0), jnp.where(low, 0.0, qc)], axis=0).astype(jnp.bfloat16)
        s = lax.dot_general(k_ref[0, pl.ds(k0, nk), c * LANES:(c + 1) * LANES], qm, (((1,), (1,)), ((), ())),
                            preferred_element_type=jnp.float32)
        s = s + bias_ref[variants[sub], c]
        return s, jnp.max(jnp.max(s.reshape(nk // 8, 8, 2 * nq), axis=0), axis=0, keepdims=True)

    def weighted(sub, c, s, m):
        kb = band_rows[sub] // (kt // GRID_W)
        accs = [jnp.zeros((LANES, nq), jnp.float32) for _ in range(2)]
        for t in range(nk // kt):
            p = jnp.exp2(s[t * kt:(t + 1) * kt] - m).astype(jnp.bfloat16)
            for half in range(2):
                accs[half] = accs[half] + jnp.dot(vt_ref[half, kb + t, c * LANES:(c + 1) * LANES, :],
                                                  p[:, half * nq:(half + 1) * nq],
                                                  preferred_element_type=jnp.float32)
        halves = []
        for half in range(2):
            other = (1 - half) * HEAD_DIM
            halves.append(accs[half][half * HEAD_DIM:(half + 1) * HEAD_DIM] / accs[half][other:other + 1])
        o_ref[0, sub * nq:(sub + 1) * nq, c * LANES:(c + 1) * LANES] = (
            jnp.concatenate(halves, axis=0).T.astype(o_ref.dtype))

    units = [(sub, c) for c in range(NA_WIDTH // LANES) for sub in range(NA_SUBSTEPS)]
    ahead = NA_SUBSTEPS
    pending = [scores(*u) for u in units[:ahead]]
    for n, u in enumerate(units):
        if n + ahead < len(units):
            pending.append(scores(*units[n + ahead]))
        weighted(*u, *pending.pop(0))


def _na_attention(proj3, bias):
    nb, seq, _ = proj3.shape
    grid_rows = seq // GRID_W
    assert grid_rows % (NA_ROWS * NA_SUBSTEPS) == 0 and grid_rows >= NA_BAND
    steps = grid_rows // (NA_ROWS * NA_SUBSTEPS)
    nq = NA_ROWS * GRID_W
    return pl.pallas_call(
        functools.partial(_na_kernel, grid_rows=grid_rows),
        grid=(nb, steps),
        in_specs=[
            pl.BlockSpec((1, NA_SUBSTEPS * nq, NA_WIDTH), lambda b, g: (b, g, 0)),
            pl.BlockSpec((1, seq, NA_WIDTH), lambda b, g: (b, 0, 1)),
            pl.BlockSpec((1, seq, NA_WIDTH), lambda b, g: (b, 0, 2)),
            pl.BlockSpec(bias.shape, lambda b, g: (0, 0, 0, 0), pipeline_mode=pl.Buffered(1)),
        ],
        out_specs=pl.BlockSpec((1, NA_SUBSTEPS * nq, NA_WIDTH), lambda b, g: (b, g, 0)),
        out_shape=jax.ShapeDtypeStruct((nb, seq, NA_WIDTH), jnp.bfloat16),
        scratch_shapes=[pltpu.VMEM((2, seq // NA_KEY_TILE, NA_WIDTH, NA_KEY_TILE), jnp.bfloat16)],
        compiler_params=pltpu.CompilerParams(dimension_semantics=("parallel", "arbitrary"),
                                             vmem_limit_bytes=VMEM_LIMIT),
        name="neighbourhood_attention",
    )(proj3, proj3, proj3, bias)


def _na_bias_tables(rpb_l):
    i = np.arange(NA_ROWS)[:, None]
    a = np.arange(NA_BAND)[None, :]
    d = a - i
    start = np.stack([np.zeros_like(d), np.zeros_like(d) + i, np.zeros_like(d) + 4])
    row_ok = (a[None] >= start) & (a[None] < start + NA_WIN_H)
    row_off = np.stack([d + 7, d + 3, d - 1])
    row_off = np.clip(row_off, 0, 2 * NA_WIN_H - 2)
    cq = np.arange(GRID_W)[:, None]
    ck = np.arange(GRID_W)[None, :]
    c0 = np.clip(cq - NA_WIN_W // 2, 0, GRID_W - NA_WIN_W)
    col_ok = (ck >= c0) & (ck < c0 + NA_WIN_W)
    col_off = np.clip(ck - cq + NA_WIN_W - 1, 0, 2 * NA_WIN_W - 2)
    cols = jnp.where(col_ok.T[None, None], rpb_l.astype(jnp.float32)[:, :, col_off.T] * LOG2_E, MASKED)
    t = jnp.stack([cols[:, r] for r in row_off.reshape(-1)], axis=1)
    t = t.reshape(NA_HEADS // 2, 2, 3, NA_ROWS, NA_BAND, GRID_W, GRID_W)
    t = jnp.where(row_ok[None, None, :, :, :, None, None], t, MASKED)
    t = jnp.transpose(t, (2, 0, 4, 5, 1, 3, 6))
    return t.reshape(3, NA_HEADS // 2, NA_BAND * GRID_W, 2 * NA_ROWS * GRID_W)


def _gqa_kernel(trips_ref, q_ref, k_ref, v_ref, o_ref, vt_ref, s0_ref, s1_ref, acc_ref, *, seq):
    tq, tk = _gqa_q_tile(seq), GQA_K_TILE
    n_chunks = seq // tk
    unroll = _gqa_unroll(seq)
    s_refs = (s0_ref, s1_ref)
    MAX_ROWS = 8

    @pl.when(pl.program_id(1) == 0)
    def _():
        ones = jnp.ones((HEAD_DIM, tk), jnp.bfloat16)
        for j in range(n_chunks):
            t = v_ref[0, j * tk:(j + 1) * tk, :].astype(jnp.float32).T
            for hk in range(GQA_KV_HEADS):
                vt_ref[hk, j, :HEAD_DIM, :] = t[hk * HEAD_DIM:(hk + 1) * HEAD_DIM].astype(jnp.bfloat16)
                vt_ref[hk, j, HEAD_DIM:, :] = ones

    low = _low_half((tq, LANES))
    heads = [(c, hk) for c in range(GQA_Q_WIDTH // LANES) for hk in range(GQA_KV_HEADS)]

    def masked_q(c, hk):
        qc = q_ref[0, :, c * LANES:(c + 1) * LANES].astype(jnp.float32)
        own = low if hk == 0 else jnp.logical_not(low)
        return jnp.where(own, qc, 0.0).astype(jnp.bfloat16)

    def scores(j, qm, slot, mx):
        r0 = pl.multiple_of(j * tk, tk)
        s = lax.dot_general(k_ref[0, pl.ds(r0, tk), :], qm, (((1,), (1,)), ((), ())),
                            preferred_element_type=jnp.float32)
        s_refs[slot][pl.ds(r0, tk), :] = s
        return jnp.maximum(mx, jnp.max(s.reshape(tk // MAX_ROWS, MAX_ROWS, tq), axis=0))

    def weighted(j, hk, slot, m):
        r0 = pl.multiple_of(j * tk, tk)
        p = jnp.exp2(s_refs[slot][pl.ds(r0, tk), :] - m).astype(jnp.bfloat16)
        acc_ref[...] += jnp.dot(vt_ref[hk, j], p, preferred_element_type=jnp.float32)

    def chunk_loop(step, init):
        def body(t, carry):
            for u in range(unroll):
                carry = step(t * unroll + u, carry)
            return carry
        return lax.fori_loop(0, trips_ref[0], body, init)

    mx0 =jnp.full((MAX_ROWS, tq), -jnp.inf, jnp.float32)
    qm = masked_q(*heads[0])
    mx = chunk_loop(lambda j, mx, qm=qm: scores(j, qm, 0, mx), mx0)
    halves = []
    for i in range(1, len(heads) + 1):
        m = jnp.max(mx, axis=0, keepdims=True)
        c_prev, hk_prev = heads[i - 1]
        cur = (i - 1) % 2
        acc_ref[...] = jnp.zeros_like(acc_ref)
        if i < len(heads):
            qm = masked_q(*heads[i])

            def both(j, mx, qm=qm, m=m, hk_prev=hk_prev, cur=cur):
                weighted(j, hk_prev, cur, m)
                return scores(j, qm, 1 - cur, mx)

            mx = chunk_loop(both, mx0)
        else:
            def last(j, carry, m=m, hk_prev=hk_prev, cur=cur):
                weighted(j, hk_prev, cur, m)
                return carry

            chunk_loop(last, 0)
        acc = acc_ref[...]
        halves.append(acc[:HEAD_DIM] / acc[HEAD_DIM:HEAD_DIM + 1])
        if hk_prev == GQA_KV_HEADS - 1:
            o_ref[0, :, c_prev * LANES:(c_prev + 1) * LANES] = jnp.concatenate(halves, axis=0).T.astype(o_ref.dtype)
            halves = []


def _gqa_unroll(seq):
    return min(GQA_MAX_UNROLL, seq // GQA_K_TILE)


def _gqa_q_tile(seq):
    return min(seq, GQA_SCORE_BYTES // (2 * 4 * seq))


def _gqa_attention(proj3):
    nb, seq, _ = proj3.shape
    tq, tk = _gqa_q_tile(seq), GQA_K_TILE
    unroll = _gqa_unroll(seq)
    assert seq % (tk * unroll) == 0 and seq % tq == 0
    q_blk = 0
    k_blk = GQA_Q_WIDTH // GQA_KV_WIDTH
    trips = jnp.full((1,), seq // (tk * unroll), jnp.int32)
    return pl.pallas_call(
        functools.partial(_gqa_kernel, seq=seq),
        grid=(nb, seq // tq),
        in_specs=[
            pl.BlockSpec(memory_space=pltpu.SMEM),
            pl.BlockSpec((1, tq, GQA_Q_WIDTH), lambda b, i: (b, i, q_blk)),
            pl.BlockSpec((1, seq, GQA_KV_WIDTH), lambda b, i: (b, 0, k_blk)),
            pl.BlockSpec((1, seq, GQA_KV_WIDTH), lambda b, i: (b, 0, k_blk + 1)),
        ],
        out_specs=pl.BlockSpec((1, tq, GQA_Q_WIDTH), lambda b, i: (b, i, 0)),
        out_shape=jax.ShapeDtypeStruct((nb, seq, GQA_Q_WIDTH), jnp.bfloat16),
        scratch_shapes=[
            pltpu.VMEM((GQA_KV_HEADS, seq // tk, LANES, tk), jnp.bfloat16),
            pltpu.VMEM((seq, tq), jnp.float32),
            pltpu.VMEM((seq, tq), jnp.float32),
            pltpu.VMEM((LANES, tq), jnp.float32),
        ],
        compiler_params=pltpu.CompilerParams(dimension_semantics=("parallel", "arbitrary"),
                                             vmem_limit_bytes=VMEM_LIMIT),
        name="gqa_attention",
    )(trips, proj3, proj3, proj3)


def _sigmoid(x):
    return 1.0 / (1.0 + jnp.exp(-x))


def _out_proj_kernel(x_ref, oa_ref, ob_ref, g_ref, scale_ref, shift_ref, gate_ref, wzg_ref, bg_ref,
                     wpa_ref, wpb_ref, wout_ref, fg_ref, o_ref, *, final):
    sub = x_ref.shape[0] // TOKEN_SUBTILES
    for t in range(TOKEN_SUBTILES):
        rows = slice(t * sub, (t + 1) * sub)
        x = x_ref[rows, :]
        hid = _modulated_norm(x, g_ref[...], scale_ref[0], shift_ref[0])
        zg = jnp.dot(hid.astype(jnp.bfloat16), wzg_ref[...], preferred_element_type=jnp.float32)
        za = zg[:, :NA_WIDTH]
        zb = zg[:, NA_WIDTH:NA_WIDTH + GQA_Q_WIDTH]
        ua = oa_ref[rows, :].astype(jnp.float32) * (za * _sigmoid(za))
        ub = ob_ref[rows, :].astype(jnp.float32) * (zb * _sigmoid(zb))
        ya = jnp.dot(ua.astype(jnp.bfloat16), wpa_ref[...], preferred_element_type=jnp.float32)
        yb = jnp.dot(ub.astype(jnp.bfloat16), wpb_ref[...], preferred_element_type=jnp.float32)
        gates = _sigmoid(zg[:, NA_WIDTH + GQA_Q_WIDTH:] + bg_ref[...])
        merged = gates[:, :D_MODEL] * ya + gates[:, D_MODEL:] * yb
        y = jnp.dot(merged.astype(jnp.bfloat16), wout_ref[...], preferred_element_type=jnp.float32)
        out = x + gate_ref[0] * y
        if final:
            ms = jnp.mean(out * out, axis=-1, keepdims=True)
            out = out * lax.rsqrt(ms + EPS) * fg_ref[...]
        o_ref[rows, :] = out


def _out_projection(x2, oa, ob, norm_g, scale, shift, gate, wzg, b_gate, wpa, wpb, wout, final_g, seq, final):
    tokens = x2.shape[0]
    tm = TOKEN_TILE
    per_seq = seq // tm
    vec = lambda i: (0, 0)
    per_batch = lambda i: (i // per_seq, 0, 0)
    return pl.pallas_call(
        functools.partial(_out_proj_kernel, final=final),
        grid=(tokens // tm,),
        in_specs=[
            pl.BlockSpec((tm, D_MODEL), lambda i: (i, 0)),
            pl.BlockSpec((tm, NA_WIDTH), lambda i: (i, 0)),
            pl.BlockSpec((tm, GQA_Q_WIDTH), lambda i: (i, 0)),
            pl.BlockSpec((1, D_MODEL), vec),
            pl.BlockSpec((1, 1, D_MODEL), per_batch),
            pl.BlockSpec((1, 1, D_MODEL), per_batch),
            pl.BlockSpec((1, 1, D_MODEL), per_batch),
            pl.BlockSpec((D_MODEL, ZG_WIDTH), vec, pipeline_mode=pl.Buffered(1)),
            pl.BlockSpec((1, 2 * D_MODEL), vec),
            pl.BlockSpec((NA_WIDTH, D_MODEL), vec, pipeline_mode=pl.Buffered(1)),
            pl.BlockSpec((GQA_Q_WIDTH, D_MODEL), vec, pipeline_mode=pl.Buffered(1)),
            pl.BlockSpec((D_MODEL, D_MODEL), vec, pipeline_mode=pl.Buffered(1)),
            pl.BlockSpec((1, D_MODEL), vec),
        ],
        out_specs=pl.BlockSpec((tm, D_MODEL), lambda i: (i, 0)),
        out_shape=jax.ShapeDtypeStruct((tokens, D_MODEL), jnp.float32),
        compiler_params=pltpu.CompilerParams(dimension_semantics=("parallel",), vmem_limit_bytes=VMEM_LIMIT),
        name="out_projection",
    )(x2, oa, ob, norm_g, scale, shift, gate, wzg, b_gate, wpa, wpb, wout, final_g)


def _rope_tables(seq):
    t = jnp.arange(seq)
    row = (t // GRID_W).astype(jnp.float32)
    col = (t % GRID_W).astype(jnp.float32)
    inv = ROPE_THETA ** (-jnp.arange(0, ROPE_AXIS_DIM, 2, dtype=jnp.float32) / ROPE_AXIS_DIM)
    ang_r, ang_c = row[:, None] * inv, col[:, None] * inv
    cos = jnp.concatenate([jnp.cos(ang_r)] * 2 + [jnp.cos(ang_c)] * 2, axis=-1)
    sin = jnp.concatenate([-jnp.sin(ang_r), jnp.sin(ang_r), -jnp.sin(ang_c), jnp.sin(ang_c)], axis=-1)
    return jnp.tile(cos, (1, 2)), jnp.tile(sin, (1, 2))


def _prepare_layer(l, norm_g, w_in, b_gate, rpb, q_norm_g, k_norm_g, w_pa, w_pb, w_out):
    bounds = np.concatenate([[0], np.cumsum(IN_SPLITS)])
    qa, ka, va, za, qb, kb, vb, zb, ga, gb = (w_in[l][:, bounds[i]:bounds[i + 1]] for i in range(10))
    bf = jnp.bfloat16

    def pair_heads(w):
        r = w.shape[0]
        w = w.reshape(r, GQA_KV_HEADS, GQA_GROUP, HEAD_DIM)
        return jnp.transpose(w, (0, 2, 1, 3)).reshape(r, GQA_Q_WIDTH)

    return dict(
        norm_g=norm_g[l][None, :],
        w_gqa=jnp.concatenate([pair_heads(qb), kb, vb], axis=1).astype(bf),
        w_na=jnp.concatenate([qa * (ATTN_SCALE * LOG2_E), ka, va], axis=1).astype(bf),
        wzg=jnp.concatenate([za, pair_heads(zb), ga, gb], axis=1).astype(bf),
        b_gate=b_gate[l][None, :],
        bias=_na_bias_tables(rpb[l]),
        qg=jnp.tile(q_norm_g[l] * (ATTN_SCALE * LOG2_E), 2)[None, :],
        kg=jnp.tile(k_norm_g[l], 2)[None, :],
        wpa=w_pa[l].astype(bf),
        wpb=jnp.transpose(w_pb[l].reshape(GQA_KV_HEADS, GQA_GROUP, HEAD_DIM, D_MODEL),
                          (1, 0, 2, 3)).reshape(GQA_Q_WIDTH, D_MODEL).astype(bf),
        wout=w_out[l].astype(bf),
    )


def _trunk(x, mod, layers, final_g):
    nb, seq, _ = x.shape
    x2 = x.reshape(nb * seq, D_MODEL)
    cos_t, sin_t = _rope_tables(seq)
    for l, p in enumerate(layers):
        shift, scale, gate = (mod[l][:, None, i * D_MODEL:(i + 1) * D_MODEL] for i in range(3))
        proj_gqa, proj_na = _in_projection(x2, p["norm_g"], scale, shift, p["w_gqa"], p["w_na"], cos_t, sin_t,
                                           p["qg"], p["kg"], seq)
        oa = _na_attention(proj_na.reshape(nb, seq, NA_PROJ_WIDTH), p["bias"]).reshape(nb * seq, NA_WIDTH)
        ob = _gqa_attention(proj_gqa.reshape(nb, seq, GQA_PROJ_WIDTH)).reshape(nb * seq, GQA_Q_WIDTH)
        x2 = _out_projection(x2, oa, ob, p["norm_g"], scale, shift, gate, p["wzg"], p["b_gate"],
                             p["wpa"], p["wpb"], p["wout"], final_g[None, :], seq, final=(l == len(layers) - 1))
    return x2.reshape(nb, seq, D_MODEL)


def kernel(x_prompt, x_sample, c_prompt, c_sample, norm_g, w_ada, b_ada, w_in, b_gate, rpb, q_norm_g, k_norm_g, w_pa, w_pb, w_out, final_g):
    layers = [_prepare_layer(l, norm_g, w_in, b_gate, rpb, q_norm_g, k_norm_g, w_pa, w_pb, w_out)
              for l in range(DEPTH)]
    n_prompt = c_prompt.shape[0]
    mod = _modulation(jnp.concatenate([c_prompt, c_sample], axis=0), w_ada, b_ada)
    y_prompt = _trunk(x_prompt, mod[:, :n_prompt], layers, final_g)
    y_sample = _trunk(x_sample, mod[:, n_prompt:], layers, final_g)
    return (y_prompt, y_sample)
```

```python
import functools

import jax
import jax.numpy as jnp
import numpy as np
from jax import lax
from jax.experimental import pallas as pl
from jax.experimental.pallas import tpu as pltpu

D_MODEL = 1024
DEPTH = 4
GRID_W = 64
HEAD_DIM = 64
NA_HEADS = 8
NA_WIDTH = NA_HEADS * HEAD_DIM
NA_WIN_H = 8
NA_WIN_W = 16
GQA_Q_HEADS = 8
GQA_KV_HEADS = 2
GQA_GROUP = GQA_Q_HEADS // GQA_KV_HEADS
GQA_Q_WIDTH = GQA_Q_HEADS * HEAD_DIM
GQA_KV_WIDTH = GQA_KV_HEADS * HEAD_DIM
ROPE_AXIS_DIM = HEAD_DIM // 2
ROPE_THETA = 10000.0
EPS = 1e-6
ATTN_SCALE = HEAD_DIM ** -0.5
IN_SPLITS = (NA_WIDTH,) * 4 + (GQA_Q_WIDTH, GQA_KV_WIDTH, GQA_KV_WIDTH, GQA_Q_WIDTH, D_MODEL, D_MODEL)
MASKED = -1e30
LOG2_E = 1.4426950408889634

LANES = 128
NA_PROJ_WIDTH = 3 * NA_WIDTH
GQA_PROJ_WIDTH = GQA_Q_WIDTH + 2 * GQA_KV_WIDTH
ZG_WIDTH = NA_WIDTH + GQA_Q_WIDTH + 2 * D_MODEL
TOKEN_TILE = 1024
TOKEN_SUBTILES = 4
NA_ROWS = 4
NA_BAND = 12
NA_KEY_TILE = NA_ROWS * GRID_W
NA_SUBSTEPS = 4
GQA_SCORE_BYTES = 16 * 1024 * 1024
GQA_K_TILE = 256
GQA_MAX_CHUNKS = 16
VMEM_LIMIT = 56 * 1024 * 1024


def _low_half(shape):
    return lax.broadcasted_iota(jnp.int32, shape, len(shape) - 1) < HEAD_DIM


def _mod_kernel(c_ref, w_ref, b_ref, o_ref):
    c = c_ref[...]
    sc = c * (1.0 / (1.0 + jnp.exp(-c)))
    o_ref[0] = jnp.dot(sc, w_ref[0], precision=lax.Precision.HIGHEST,
                       preferred_element_type=jnp.float32) + b_ref[0]


def _modulation(c_all, w_ada, b_ada):
    nb = c_all.shape[0]
    n_chunks = 3
    return pl.pallas_call(
        _mod_kernel,
        grid=(DEPTH, n_chunks),
        in_specs=[
            pl.BlockSpec((nb, D_MODEL), lambda l, j: (0, 0)),
            pl.BlockSpec((1, D_MODEL, D_MODEL), lambda l, j: (l, 0, j)),
            pl.BlockSpec((1, 1, D_MODEL), lambda l, j: (l, 0, j)),
        ],
        out_specs=pl.BlockSpec((1, nb, D_MODEL), lambda l, j: (l, 0, j)),
        out_shape=jax.ShapeDtypeStruct((DEPTH, nb, 3 * D_MODEL), jnp.float32),
        compiler_params=pltpu.CompilerParams(vmem_limit_bytes=VMEM_LIMIT),
        name="adaln_modulation",
    )(c_all, w_ada, b_ada.reshape(DEPTH, 1, 3 * D_MODEL))


def _modulated_norm(x, g, scale, shift):
    ms = jnp.mean(x * x, axis=-1, keepdims=True)
    y = x * lax.rsqrt(ms + EPS) * g
    return y * (1.0 + scale) + shift


def _head_norm_rope(x, g, cos, sin_signed, low):
    sq = x * x
    lo = jnp.sum(jnp.where(low, sq, 0.0), axis=-1, keepdims=True)
    hi = jnp.sum(jnp.where(low, 0.0, sq), axis=-1, keepdims=True)
    ms = jnp.where(low, lo, hi) * (1.0 / HEAD_DIM)
    y = x * lax.rsqrt(ms + EPS) * g
    lane = lax.broadcasted_iota(jnp.int32, x.shape, 1)
    quarter = ROPE_AXIS_DIM // 2
    partner = jnp.where((lane & quarter) != 0, pltpu.roll(y, quarter, 1), pltpu.roll(y, LANES - quarter, 1))
    return y * cos + partner * sin_signed


def _in_proj_kernel(x_ref, g_ref, scale_ref, shift_ref, wb_ref, wa_ref, cos_ref, sin_ref, qg_ref, kg_ref,
                    ob_ref, oa_ref):
    n_q_chunks = GQA_Q_WIDTH // LANES
    v0 = GQA_Q_WIDTH + GQA_KV_WIDTH
    sub = x_ref.shape[0] // TOKEN_SUBTILES
    low = _low_half((sub, LANES))
    for t in range(TOKEN_SUBTILES):
        rows = slice(t * sub, (t + 1) * sub)
        hid = _modulated_norm(x_ref[rows, :], g_ref[...], scale_ref[0], shift_ref[0]).astype(jnp.bfloat16)
        acc = jnp.dot(hid, wb_ref[...], preferred_element_type=jnp.float32)
        cos = cos_ref[rows, :]
        sin = sin_ref[rows, :]
        for j in range(n_q_chunks + 1):
            c0 = j * LANES
            gain = qg_ref[...] if j < n_q_chunks else kg_ref[...]
            ob_ref[rows, c0:c0 + LANES] = _head_norm_rope(acc[:, c0:c0 + LANES], gain, cos, sin,
                                                          low).astype(ob_ref.dtype)
        ob_ref[rows, v0:] = acc[:, v0:].astype(ob_ref.dtype)
        oa_ref[rows, :] = jnp.dot(hid, wa_ref[...], preferred_element_type=jnp.float32).astype(oa_ref.dtype)


def _in_projection(x2, norm_g, scale, shift, wb, wa, cos_t, sin_t, qg, kg, seq):
    tokens = x2.shape[0]
    tm = TOKEN_TILE
    per_seq = seq // tm
    vec = lambda i: (0, 0)
    return pl.pallas_call(
        _in_proj_kernel,
        grid=(tokens // tm,),
        in_specs=[
            pl.BlockSpec((tm, D_MODEL), lambda i: (i, 0)),
            pl.BlockSpec((1, D_MODEL), vec),
            pl.BlockSpec((1, 1, D_MODEL), lambda i: (i // per_seq, 0, 0)),
            pl.BlockSpec((1, 1, D_MODEL), lambda i: (i // per_seq, 0, 0)),
            pl.BlockSpec((D_MODEL, GQA_PROJ_WIDTH), vec, pipeline_mode=pl.Buffered(1)),
            pl.BlockSpec((D_MODEL, NA_PROJ_WIDTH), vec, pipeline_mode=pl.Buffered(1)),
            pl.BlockSpec((tm, LANES), lambda i: (i % per_seq, 0)),
            pl.BlockSpec((tm, LANES), lambda i: (i % per_seq, 0)),
            pl.BlockSpec((1, LANES), vec),
            pl.BlockSpec((1, LANES), vec),
        ],
        out_specs=[pl.BlockSpec((tm, GQA_PROJ_WIDTH), lambda i: (i, 0)),
                   pl.BlockSpec((tm, NA_PROJ_WIDTH), lambda i: (i, 0))],
        out_shape=[jax.ShapeDtypeStruct((tokens, GQA_PROJ_WIDTH), jnp.bfloat16),
                   jax.ShapeDtypeStruct((tokens, NA_PROJ_WIDTH), jnp.bfloat16)],
        compiler_params=pltpu.CompilerParams(dimension_semantics=("parallel",), vmem_limit_bytes=VMEM_LIMIT),
        name="in_projection",
    )(x2, norm_g, scale, shift, wb, wa, cos_t, sin_t, qg, kg)


def _na_kernel(q_ref, k_ref, v_ref, bias_ref, o_ref, vt_ref, *, grid_rows):
    step = pl.program_id(1)
    seq = grid_rows * GRID_W
    nk = NA_BAND * GRID_W
    nq = NA_ROWS * GRID_W
    kt = NA_KEY_TILE
    last_group = grid_rows // NA_ROWS - 1

    @pl.when(step == 0)
    def _():
        row = lax.broadcasted_iota(jnp.int32, (NA_WIDTH, kt), 0)
        even_rows = (row & HEAD_DIM) == 0
        for j in range(seq // kt):
            t = v_ref[0, j * kt:(j + 1) * kt, :].astype(jnp.float32).T
            vt_ref[0, j] = jnp.where(even_rows, t, 1.0).astype(jnp.bfloat16)
            vt_ref[1, j] = jnp.where(even_rows, 1.0, t).astype(jnp.bfloat16)

    low = _low_half((nq, LANES))
    groups = [NA_SUBSTEPS * step + sub for sub in range(NA_SUBSTEPS)]
    band_rows = [jnp.clip(NA_ROWS * g - NA_WIN_H // 2, 0, grid_rows - NA_BAND) for g in groups]
    variants = [(g > 0).astype(jnp.int32) + (g == last_group).astype(jnp.int32) for g in groups]

    def scores(sub, c):
        k0 = pl.multiple_of(band_rows[sub] * GRID_W, kt)
        qc = q_ref[0, sub * nq:(sub + 1) * nq, c * LANES:(c + 1) * LANES].astype(jnp.float32)
        qm = jnp.concatenate([jnp.where(low, qc, 0.0), jnp.where(low, 0.0, qc)], axis=0).astype(jnp.bfloat16)
        s = lax.dot_general(k_ref[0, pl.ds(k0, nk), c * LANES:(c + 1) * LANES], qm, (((1,), (1,)), ((), ())),
                            preferred_element_type=jnp.float32)
        s = s + bias_ref[variants[sub], c]
        return s, jnp.max(jnp.max(s.reshape(nk // 8, 8, 2 * nq), axis=0), axis=0, keepdims=True)

    def weighted(sub, c, s, m):
        kb = band_rows[sub] // (kt // GRID_W)
        accs = [jnp.zeros((LANES, nq), jnp.float32) for _ in range(2)]
        for t in range(nk // kt):
            p = jnp.exp2(s[t * kt:(t + 1) * kt] - m).astype(jnp.bfloat16)
            for half in range(2):
                accs[half] = accs[half] + jnp.dot(vt_ref[half, kb + t, c * LANES:(c + 1) * LANES, :],
                                                  p[:, half * nq:(half + 1) * nq],
                                                  preferred_element_type=jnp.float32)
        halves = []
        for half in range(2):
            other = (1 - half) * HEAD_DIM
            halves.append(accs[half][half * HEAD_DIM:(half + 1) * HEAD_DIM] / accs[half][other:other + 1])
        o_ref[0, sub * nq:(sub + 1) * nq, c * LANES:(c + 1) * LANES] = (
            jnp.concatenate(halves, axis=0).T.astype(o_ref.dtype))

    units = [(sub, c) for c in range(NA_WIDTH // LANES) for sub in range(NA_SUBSTEPS)]
    ahead = NA_SUBSTEPS
    pending = [scores(*u) for u in units[:ahead]]
    for n, u in enumerate(units):
        if n + ahead < len(units):
            pending.append(scores(*units[n + ahead]))
        weighted(*u, *pending.pop(0))


def _na_attention(proj3, bias):
    nb, seq, _ = proj3.shape
    grid_rows = seq // GRID_W
    assert grid_rows % (NA_ROWS * NA_SUBSTEPS) == 0 and grid_rows >= NA_BAND
    steps = grid_rows // (NA_ROWS * NA_SUBSTEPS)
    nq = NA_ROWS * GRID_W
    return pl.pallas_call(
        functools.partial(_na_kernel, grid_rows=grid_rows),
        grid=(nb, steps),
        in_specs=[
            pl.BlockSpec((1, NA_SUBSTEPS * nq, NA_WIDTH), lambda b, g: (b, g, 0)),
            pl.BlockSpec((1, seq, NA_WIDTH), lambda b, g: (b, 0, 1)),
            pl.BlockSpec((1, seq, NA_WIDTH), lambda b, g: (b, 0, 2)),
            pl.BlockSpec(bias.shape, lambda b, g: (0, 0, 0, 0), pipeline_mode=pl.Buffered(1)),
        ],
        out_specs=pl.BlockSpec((1, NA_SUBSTEPS * nq, NA_WIDTH), lambda b, g: (b, g, 0)),
        out_shape=jax.ShapeDtypeStruct((nb, seq, NA_WIDTH), jnp.bfloat16),
        scratch_shapes=[pltpu.VMEM((2, seq // NA_KEY_TILE, NA_WIDTH, NA_KEY_TILE), jnp.bfloat16)],
        compiler_params=pltpu.CompilerParams(dimension_semantics=("parallel", "arbitrary"),
                                             vmem_limit_bytes=VMEM_LIMIT),
        name="neighbourhood_attention",
    )(proj3, proj3, proj3, bias)


def _na_bias_tables(rpb_l):
    i = np.arange(NA_ROWS)[:, None]
    a = np.arange(NA_BAND)[None, :]
    d = a - i
    start = np.stack([np.zeros_like(d), np.zeros_like(d) + i, np.zeros_like(d) + 4])
    row_ok = (a[None] >= start) & (a[None] < start + NA_WIN_H)
    row_off = np.stack([d + 7, d + 3, d - 1])
    row_off = np.clip(row_off, 0, 2 * NA_WIN_H - 2)
    cq = np.arange(GRID_W)[:, None]
    ck = np.arange(GRID_W)[None, :]
    c0 = np.clip(cq - NA_WIN_W // 2, 0, GRID_W - NA_WIN_W)
    col_ok = (ck >= c0) & (ck < c0 + NA_WIN_W)
    col_off = np.clip(ck - cq + NA_WIN_W - 1, 0, 2 * NA_WIN_W - 2)
    cols = jnp.where(col_ok.T[None, None], rpb_l.astype(jnp.float32)[:, :, col_off.T] * LOG2_E, MASKED)
    t = jnp.stack([cols[:, r] for r in row_off.reshape(-1)], axis=1)
    t = t.reshape(NA_HEADS // 2, 2, 3, NA_ROWS, NA_BAND, GRID_W, GRID_W)
    t = jnp.where(row_ok[None, None, :, :, :, None, None], t, MASKED)
    t = jnp.transpose(t, (2, 0, 4, 5, 1, 3, 6))
    return t.reshape(3, NA_HEADS // 2, NA_BAND * GRID_W, 2 * NA_ROWS * GRID_W)


def _gqa_kernel(trips_ref, q_ref, k_ref, v_ref, o_ref, vt_ref, s0_ref, s1_ref, acc0_ref, acc1_ref,
                qm0_ref, qm1_ref, stage_ref, *, seq):
    tq, tk = _gqa_q_tile(seq), GQA_K_TILE
    n_chunks = seq // tk
    s_refs = (s0_ref, s1_ref)
    acc_refs = (acc0_ref, acc1_ref)
    qm_refs = (qm0_ref, qm1_ref)
    MAX_ROWS = 8

    @pl.when(pl.program_id(1) == 0)
    def _():
        ones = jnp.ones((HEAD_DIM, tk), jnp.bfloat16)
        for j in range(n_chunks):
            t = v_ref[0, j * tk:(j + 1) * tk, :].astype(jnp.float32).T
            for hk in range(GQA_KV_HEADS):
                vt_ref[hk, j, :HEAD_DIM, :] = t[hk * HEAD_DIM:(hk + 1) * HEAD_DIM].astype(jnp.bfloat16)
                vt_ref[hk, j, HEAD_DIM:, :] = ones

    low = _low_half((tq, LANES))
    heads = [(c, hk) for c in range(GQA_Q_WIDTH // LANES) for hk in range(GQA_KV_HEADS)]

    def masked_q(c, hk):
        qc = q_ref[0, :, c * LANES:(c + 1) * LANES].astype(jnp.float32)
        own = low if hk == 0 else jnp.logical_not(low)
        return jnp.where(own, qc, 0.0).astype(jnp.bfloat16)

    def scores(j, qm, slot, mx):
        r0 = j * tk
        s = lax.dot_general(k_ref[0, pl.ds(r0, tk), :], qm, (((1,), (1,)), ((), ())),
                            preferred_element_type=jnp.float32)
        s_refs[slot][pl.ds(r0, tk), :] = s
        return jnp.maximum(mx, jnp.max(s.reshape(tk // MAX_ROWS, MAX_ROWS, tq), axis=0))

    def weighted(j, h, m):
        r0 = j * tk
        p = jnp.exp2(s_refs[h % 2][pl.ds(r0, tk), :] - m).astype(jnp.bfloat16)
        part = jnp.dot(vt_ref[heads[h][1], j], p, preferred_element_type=jnp.float32)
        if j == 0:
            acc_refs[h % 2][...] = part
        else:
            acc_refs[h % 2][...] += part

    def finish(h):
        c, hk = heads[h]
        acc = acc_refs[h % 2][...]
        stage_ref[hk * HEAD_DIM:(hk + 1) * HEAD_DIM, :] = acc[:HEAD_DIM] / acc[HEAD_DIM:HEAD_DIM + 1]
        if hk == GQA_KV_HEADS - 1:
            o_ref[0, :, c * LANES:(c + 1) * LANES] = stage_ref[...].T.astype(o_ref.dtype)

    n_heads = len(heads)
    mx0 = jnp.full((MAX_ROWS, tq), -jnp.inf, jnp.float32)
    qm_refs[0][...] = masked_q(*heads[0])
    m = None
    for i in range(n_heads + 1):
        def body(mx, i=i, m=m):
            if i >= 2:
                finish(i - 2)
            if i + 1 < n_heads:
                qm_refs[(i + 1) % 2][...] = masked_q(*heads[i + 1])
            qm = qm_refs[i % 2][...] if i < n_heads else None
            for j in range(n_chunks):
                if i >= 1:
                    weighted(j, i - 1, m)
                if i < n_heads:
                    mx = scores(j, qm, i % 2, mx)
            return mx

        mx = lax.fori_loop(0, trips_ref[0], lambda t, mx, body=body: body(mx), mx0)
        m = jnp.max(mx, axis=0, keepdims=True)
    finish(n_heads - 1)


def _gqa_q_tile(seq):
    return min(seq, GQA_SCORE_BYTES // (2 * 4 * seq))


def _gqa_attention(proj3):
    nb, seq, _ = proj3.shape
    tq, tk = _gqa_q_tile(seq), GQA_K_TILE
    assert seq % tk == 0 and seq // tk <= GQA_MAX_CHUNKS and seq % tq == 0
    q_blk = 0
    k_blk = GQA_Q_WIDTH // GQA_KV_WIDTH
    trips = jnp.ones((1,), jnp.int32)
    return pl.pallas_call(
        functools.partial(_gqa_kernel, seq=seq),
        grid=(nb, seq // tq),
        in_specs=[
            pl.BlockSpec(memory_space=pltpu.SMEM),
            pl.BlockSpec((1, tq, GQA_Q_WIDTH), lambda b, i: (b, i, q_blk)),
            pl.BlockSpec((1, seq, GQA_KV_WIDTH), lambda b, i: (b, 0, k_blk)),
            pl.BlockSpec((1, seq, GQA_KV_WIDTH), lambda b, i: (b, 0, k_blk + 1)),
        ],
        out_specs=pl.BlockSpec((1, tq, GQA_Q_WIDTH), lambda b, i: (b, i, 0)),
        out_shape=jax.ShapeDtypeStruct((nb, seq, GQA_Q_WIDTH), jnp.bfloat16),
        scratch_shapes=[
            pltpu.VMEM((GQA_KV_HEADS, seq // tk, LANES, tk), jnp.bfloat16),
            pltpu.VMEM((seq, tq), jnp.float32),
            pltpu.VMEM((seq, tq), jnp.float32),
            pltpu.VMEM((LANES, tq), jnp.float32),
            pltpu.VMEM((LANES, tq), jnp.float32),
            pltpu.VMEM((tq, LANES), jnp.bfloat16),
            pltpu.VMEM((tq, LANES), jnp.bfloat16),
            pltpu.VMEM((LANES, tq), jnp.float32),
        ],
        compiler_params=pltpu.CompilerParams(dimension_semantics=("parallel", "arbitrary"),
                                             vmem_limit_bytes=VMEM_LIMIT),
        name="gqa_attention",
    )(trips, proj3, proj3, proj3)


def _sigmoid(x):
    return 1.0 / (1.0 + jnp.exp(-x))


def _out_proj_kernel(x_ref, oa_ref, ob_ref, g_ref, scale_ref, shift_ref, gate_ref, wzg_ref, bg_ref,
                     wpa_ref, wpb_ref, wout_ref, fg_ref, o_ref, *, final):
    sub = x_ref.shape[0] // TOKEN_SUBTILES
    for t in range(TOKEN_SUBTILES):
        rows = slice(t * sub, (t + 1) * sub)
        x = x_ref[rows, :]
        hid = _modulated_norm(x, g_ref[...], scale_ref[0], shift_ref[0])
        zg = jnp.dot(hid.astype(jnp.bfloat16), wzg_ref[...], preferred_element_type=jnp.float32)
        za = zg[:, :NA_WIDTH]
        zb = zg[:, NA_WIDTH:NA_WIDTH + GQA_Q_WIDTH]
        ua = oa_ref[rows, :].astype(jnp.float32) * (za * _sigmoid(za))
        ub = ob_ref[rows, :].astype(jnp.float32) * (zb * _sigmoid(zb))
        ya = jnp.dot(ua.astype(jnp.bfloat16), wpa_ref[...], preferred_element_type=jnp.float32)
        yb = jnp.dot(ub.astype(jnp.bfloat16), wpb_ref[...], preferred_element_type=jnp.float32)
        gates = _sigmoid(zg[:, NA_WIDTH + GQA_Q_WIDTH:] + bg_ref[...])
        merged = gates[:, :D_MODEL] * ya + gates[:, D_MODEL:] * yb
        y = jnp.dot(merged.astype(jnp.bfloat16), wout_ref[...], preferred_element_type=jnp.float32)
        out = x + gate_ref[0] * y
        if final:
            ms = jnp.mean(out * out, axis=-1, keepdims=True)
            out = out * lax.rsqrt(ms + EPS) * fg_ref[...]
        o_ref[rows, :] = out


def _out_projection(x2, oa, ob, norm_g, scale, shift, gate, wzg, b_gate, wpa, wpb, wout, final_g, seq, final):
    tokens = x2.shape[0]
    tm = TOKEN_TILE
    per_seq = seq // tm
    vec = lambda i: (0, 0)
    per_batch = lambda i: (i // per_seq, 0, 0)
    return pl.pallas_call(
        functools.partial(_out_proj_kernel, final=final),
        grid=(tokens // tm,),
        in_specs=[
            pl.BlockSpec((tm, D_MODEL), lambda i: (i, 0)),
            pl.BlockSpec((tm, NA_WIDTH), lambda i: (i, 0)),
            pl.BlockSpec((tm, GQA_Q_WIDTH), lambda i: (i, 0)),
            pl.BlockSpec((1, D_MODEL), vec),
            pl.BlockSpec((1, 1, D_MODEL), per_batch),
            pl.BlockSpec((1, 1, D_MODEL), per_batch),
            pl.BlockSpec((1, 1, D_MODEL), per_batch),
            pl.BlockSpec((D_MODEL, ZG_WIDTH), vec, pipeline_mode=pl.Buffered(1)),
            pl.BlockSpec((1, 2 * D_MODEL), vec),
            pl.BlockSpec((NA_WIDTH, D_MODEL), vec, pipeline_mode=pl.Buffered(1)),
            pl.BlockSpec((GQA_Q_WIDTH, D_MODEL), vec, pipeline_mode=pl.Buffered(1)),
            pl.BlockSpec((D_MODEL, D_MODEL), vec, pipeline_mode=pl.Buffered(1)),
            pl.BlockSpec((1, D_MODEL), vec),
        ],
        out_specs=pl.BlockSpec((tm, D_MODEL), lambda i: (i, 0)),
        out_shape=jax.ShapeDtypeStruct((tokens, D_MODEL), jnp.float32),
        compiler_params=pltpu.CompilerParams(dimension_semantics=("parallel",), vmem_limit_bytes=VMEM_LIMIT),
        name="out_projection",
    )(x2, oa, ob, norm_g, scale, shift, gate, wzg, b_gate, wpa, wpb, wout, final_g)


def _rope_tables(seq):
    t = jnp.arange(seq)
    row = (t // GRID_W).astype(jnp.float32)
    col = (t % GRID_W).astype(jnp.float32)
    inv = ROPE_THETA ** (-jnp.arange(0, ROPE_AXIS_DIM, 2, dtype=jnp.float32) / ROPE_AXIS_DIM)
    ang_r, ang_c = row[:, None] * inv, col[:, None] * inv
    cos = jnp.concatenate([jnp.cos(ang_r)] * 2 + [jnp.cos(ang_c)] * 2, axis=-1)
    sin = jnp.concatenate([-jnp.sin(ang_r), jnp.sin(ang_r), -jnp.sin(ang_c), jnp.sin(ang_c)], axis=-1)
    return jnp.tile(cos, (1, 2)), jnp.tile(sin, (1, 2))


def _prepare_layer(l, norm_g, w_in, b_gate, rpb, q_norm_g, k_norm_g, w_pa, w_pb, w_out):
    bounds = np.concatenate([[0], np.cumsum(IN_SPLITS)])
    qa, ka, va, za, qb, kb, vb, zb, ga, gb = (w_in[l][:, bounds[i]:bounds[i + 1]] for i in range(10))
    bf = jnp.bfloat16

    def pair_heads(w):
        r = w.shape[0]
        w = w.reshape(r, GQA_KV_HEADS, GQA_GROUP, HEAD_DIM)
        return jnp.transpose(w, (0, 2, 1, 3)).reshape(r, GQA_Q_WIDTH)

    return dict(
        norm_g=norm_g[l][None, :],
        w_gqa=jnp.concatenate([pair_heads(qb), kb, vb], axis=1).astype(bf),
        w_na=jnp.concatenate([qa * (ATTN_SCALE * LOG2_E), ka, va], axis=1).astype(bf),
        wzg=jnp.concatenate([za, pair_heads(zb), ga, gb], axis=1).astype(bf),
        b_gate=b_gate[l][None, :],
        bias=_na_bias_tables(rpb[l]),
        qg=jnp.tile(q_norm_g[l] * (ATTN_SCALE * LOG2_E), 2)[None, :],
        kg=jnp.tile(k_norm_g[l], 2)[None, :],
        wpa=w_pa[l].astype(bf),
        wpb=jnp.transpose(w_pb[l].reshape(GQA_KV_HEADS, GQA_GROUP, HEAD_DIM, D_MODEL),
                          (1, 0, 2, 3)).reshape(GQA_Q_WIDTH, D_MODEL).astype(bf),
        wout=w_out[l].astype(bf),
    )


def _trunk(x, mod, layers, final_g):
    nb, seq, _ = x.shape
    x2 = x.reshape(nb * seq, D_MODEL)
    cos_t, sin_t = _rope_tables(seq)
    for l, p in enumerate(layers):
        shift, scale, gate = (mod[l][:, None, i * D_MODEL:(i + 1) * D_MODEL] for i in range(3))
        proj_gqa, proj_na = _in_projection(x2, p["norm_g"], scale, shift, p["w_gqa"], p["w_na"], cos_t, sin_t,
                                           p["qg"], p["kg"], seq)
        oa = _na_attention(proj_na.reshape(nb, seq, NA_PROJ_WIDTH), p["bias"]).reshape(nb * seq, NA_WIDTH)
        ob = _gqa_attention(proj_gqa.reshape(nb, seq, GQA_PROJ_WIDTH)).reshape(nb * seq, GQA_Q_WIDTH)
        x2 = _out_projection(x2, oa, ob, p["norm_g"], scale, shift, gate, p["wzg"], p["b_gate"],
                             p["wpa"], p["wpb"], p["wout"], final_g[None, :], seq, final=(l == len(layers) - 1))
    return x2.reshape(nb, seq, D_MODEL)


def kernel(x_prompt, x_sample, c_prompt, c_sample, norm_g, w_ada, b_ada, w_in, b_gate, rpb, q_norm_g, k_norm_g, w_pa, w_pb, w_out, final_g):
    layers = [_prepare_layer(l, norm_g, w_in, b_gate, rpb, q_norm_g, k_norm_g, w_pa, w_pb, w_out)
              for l in range(DEPTH)]
    n_prompt = c_prompt.shape[0]
    mod = _modulation(jnp.concatenate([c_prompt, c_sample], axis=0), w_ada, b_ada)
    y_prompt = _trunk(x_prompt, mod[:, :n_prompt], layers, final_g)
    y_sample = _trunk(x_sample, mod[:, n_prompt:], layers, final_g)
    return (y_prompt, y_sample)
```

```python
import functools

import jax
import jax.numpy as jnp
import numpy as np
from jax import lax
from jax.experimental import pallas as pl
from jax.experimental.pallas import tpu as pltpu

D_MODEL = 1024
DEPTH = 4
GRID_W = 64
HEAD_DIM = 64
NA_HEADS = 8
NA_WIDTH = NA_HEADS * HEAD_DIM
NA_WIN_H = 8
NA_WIN_W = 16
GQA_Q_HEADS = 8
GQA_KV_HEADS = 2
GQA_GROUP = GQA_Q_HEADS // GQA_KV_HEADS
GQA_Q_WIDTH = GQA_Q_HEADS * HEAD_DIM
GQA_KV_WIDTH = GQA_KV_HEADS * HEAD_DIM
ROPE_AXIS_DIM = HEAD_DIM // 2
ROPE_THETA = 10000.0
EPS = 1e-6
ATTN_SCALE = HEAD_DIM ** -0.5
IN_SPLITS = (NA_WIDTH,) * 4 + (GQA_Q_WIDTH, GQA_KV_WIDTH, GQA_KV_WIDTH, GQA_Q_WIDTH, D_MODEL, D_MODEL)
MASKED = -1e30
LOG2_E = 1.4426950408889634

LANES = 128
NA_PROJ_WIDTH = 3 * NA_WIDTH
GQA_PROJ_WIDTH = GQA_Q_WIDTH + 2 * GQA_KV_WIDTH
ZG_WIDTH = NA_WIDTH + GQA_Q_WIDTH + 2 * D_MODEL
TOKEN_TILE = 1024
TOKEN_SUBTILES = 4
NA_ROWS = 4
NA_BAND = 12
NA_KEY_TILE = NA_ROWS * GRID_W
NA_SUBSTEPS = 4
GQA_SCORE_BYTES = 32 * 1024 * 1024
GQA_K_TILE = 256
GQA_MAX_Q_TILE = 1024
GQA_MAX_CHUNKS = 16
VMEM_LIMIT = 56 * 1024 * 1024


def _low_half(shape):
    return lax.broadcasted_iota(jnp.int32, shape, len(shape) - 1) < HEAD_DIM


def _mod_kernel(c_ref, w_ref, b_ref, o_ref):
    c = c_ref[...]
    sc = c * (1.0 / (1.0 + jnp.exp(-c)))
    o_ref[0] = jnp.dot(sc, w_ref[0], precision=lax.Precision.HIGHEST,
                       preferred_element_type=jnp.float32) + b_ref[0]


def _modulation(c_all, w_ada, b_ada):
    nb = c_all.shape[0]
    n_chunks = 3
    return pl.pallas_call(
        _mod_kernel,
        grid=(DEPTH, n_chunks),
        in_specs=[
            pl.BlockSpec((nb, D_MODEL), lambda l, j: (0, 0)),
            pl.BlockSpec((1, D_MODEL, D_MODEL), lambda l, j: (l, 0, j)),
            pl.BlockSpec((1, 1, D_MODEL), lambda l, j: (l, 0, j)),
        ],
        out_specs=pl.BlockSpec((1, nb, D_MODEL), lambda l, j: (l, 0, j)),
        out_shape=jax.ShapeDtypeStruct((DEPTH, nb, 3 * D_MODEL), jnp.float32),
        compiler_params=pltpu.CompilerParams(vmem_limit_bytes=VMEM_LIMIT),
        name="adaln_modulation",
    )(c_all, w_ada, b_ada.reshape(DEPTH, 1, 3 * D_MODEL))


def _modulated_norm(x, g, scale, shift):
    ms = jnp.mean(x * x, axis=-1, keepdims=True)
    y = x * lax.rsqrt(ms + EPS) * g
    return y * (1.0 + scale) + shift


def _head_norm_rope(x, g, cos, sin_signed, low):
    sq = x * x
    lo = jnp.sum(jnp.where(low, sq, 0.0), axis=-1, keepdims=True)
    hi = jnp.sum(jnp.where(low, 0.0, sq), axis=-1, keepdims=True)
    ms = jnp.where(low, lo, hi) * (1.0 / HEAD_DIM)
    y = x * lax.rsqrt(ms + EPS) * g
    lane = lax.broadcasted_iota(jnp.int32, x.shape, 1)
    quarter = ROPE_AXIS_DIM // 2
    partner = jnp.where((lane & quarter) != 0, pltpu.roll(y, quarter, 1), pltpu.roll(y, LANES - quarter, 1))
    return y * cos + partner * sin_signed


def _in_proj_kernel(x_ref, g_ref, scale_ref, shift_ref, wb_ref, wa_ref, cos_ref, sin_ref, qg_ref, kg_ref,
                    ob_ref, oa_ref):
    n_q_chunks = GQA_Q_WIDTH // LANES
    v0 = GQA_Q_WIDTH + GQA_KV_WIDTH
    sub = x_ref.shape[0] // TOKEN_SUBTILES
    low = _low_half((sub, LANES))
    for t in range(TOKEN_SUBTILES):
        rows = slice(t * sub, (t + 1) * sub)
        hid = _modulated_norm(x_ref[rows, :], g_ref[...], scale_ref[0], shift_ref[0]).astype(jnp.bfloat16)
        acc = jnp.dot(hid, wb_ref[...], preferred_element_type=jnp.float32)
        cos = cos_ref[rows, :]
        sin = sin_ref[rows, :]
        for j in range(n_q_chunks + 1):
            c0 = j * LANES
            gain = qg_ref[...] if j < n_q_chunks else kg_ref[...]
            ob_ref[rows, c0:c0 + LANES] = _head_norm_rope(acc[:, c0:c0 + LANES], gain, cos, sin,
                                                          low).astype(ob_ref.dtype)
        ob_ref[rows, v0:] = acc[:, v0:].astype(ob_ref.dtype)
        oa_ref[rows, :] = jnp.dot(hid, wa_ref[...], preferred_element_type=jnp.float32).astype(oa_ref.dtype)


def _in_projection(x2, norm_g, scale, shift, wb, wa, cos_t, sin_t, qg, kg, seq):
    tokens = x2.shape[0]
    tm = TOKEN_TILE
    per_seq = seq // tm
    vec = lambda i: (0, 0)
    return pl.pallas_call(
        _in_proj_kernel,
        grid=(tokens // tm,),
        in_specs=[
            pl.BlockSpec((tm, D_MODEL), lambda i: (i, 0)),
            pl.BlockSpec((1, D_MODEL), vec),
            pl.BlockSpec((1, 1, D_MODEL), lambda i: (i // per_seq, 0, 0)),
            pl.BlockSpec((1, 1, D_MODEL), lambda i: (i // per_seq, 0, 0)),
            pl.BlockSpec((D_MODEL, GQA_PROJ_WIDTH), vec, pipeline_mode=pl.Buffered(1)),
            pl.BlockSpec((D_MODEL, NA_PROJ_WIDTH), vec, pipeline_mode=pl.Buffered(1)),
            pl.BlockSpec((tm, LANES), lambda i: (i % per_seq, 0)),
            pl.BlockSpec((tm, LANES), lambda i: (i % per_seq, 0)),
            pl.BlockSpec((1, LANES), vec),
            pl.BlockSpec((1, LANES), vec),
        ],
        out_specs=[pl.BlockSpec((tm, GQA_PROJ_WIDTH), lambda i: (i, 0)),
                   pl.BlockSpec((tm, NA_PROJ_WIDTH), lambda i: (i, 0))],
        out_shape=[jax.ShapeDtypeStruct((tokens, GQA_PROJ_WIDTH), jnp.bfloat16),
                   jax.ShapeDtypeStruct((tokens, NA_PROJ_WIDTH), jnp.bfloat16)],
        compiler_params=pltpu.CompilerParams(dimension_semantics=("parallel",), vmem_limit_bytes=VMEM_LIMIT),
        name="in_projection",
    )(x2, norm_g, scale, shift, wb, wa, cos_t, sin_t, qg, kg)


def _na_kernel(q_ref, k_ref, v_ref, bias_ref, o_ref, vt_ref, *, grid_rows):
    step = pl.program_id(1)
    seq = grid_rows * GRID_W
    nk = NA_BAND * GRID_W
    nq = NA_ROWS * GRID_W
    kt = NA_KEY_TILE
    last_group = grid_rows // NA_ROWS - 1

    @pl.when(step == 0)
    def _():
        row = lax.broadcasted_iota(jnp.int32, (NA_WIDTH, kt), 0)
        even_rows = (row & HEAD_DIM) == 0
        for j in range(seq // kt):
            t = v_ref[0, j * kt:(j + 1) * kt, :].astype(jnp.float32).T
            vt_ref[0, j] = jnp.where(even_rows, t, 1.0).astype(jnp.bfloat16)
            vt_ref[1, j] = jnp.where(even_rows, 1.0, t).astype(jnp.bfloat16)

    low = _low_half((nq, LANES))
    groups = [NA_SUBSTEPS * step + sub for sub in range(NA_SUBSTEPS)]
    band_rows = [jnp.clip(NA_ROWS * g - NA_WIN_H // 2, 0, grid_rows - NA_BAND) for g in groups]
    variants = [(g > 0).astype(jnp.int32) + (g == last_group).astype(jnp.int32) for g in groups]

    def scores(sub, c):
        k0 = pl.multiple_of(band_rows[sub] * GRID_W, kt)
        qc = q_ref[0, sub * nq:(sub + 1) * nq, c * LANES:(c + 1) * LANES].astype(jnp.float32)
        qm = jnp.concatenate([jnp.where(low, qc, 0.0), jnp.where(low, 0.0, qc)], axis=0).astype(jnp.bfloat16)
        s = lax.dot_general(k_ref[0, pl.ds(k0, nk), c * LANES:(c + 1) * LANES], qm, (((1,), (1,)), ((), ())),
                            preferred_element_type=jnp.float32)
        s = s + bias_ref[variants[sub], c]
        return s, jnp.max(jnp.max(s.reshape(nk // 8, 8, 2 * nq), axis=0), axis=0, keepdims=True)

    def weighted(sub, c, s, m):
        kb = band_rows[sub] // (kt // GRID_W)
        accs = [jnp.zeros((LANES, nq), jnp.float32) for _ in range(2)]
        for t in range(nk // kt):
            p = jnp.exp2(s[t * kt:(t + 1) * kt] - m).astype(jnp.bfloat16)
            for half in range(2):
                accs[half] = accs[half] + jnp.dot(vt_ref[half, kb + t, c * LANES:(c + 1) * LANES, :],
                                                  p[:, half * nq:(half + 1) * nq],
                                                  preferred_element_type=jnp.float32)
        halves = []
        for half in range(2):
            other = (1 - half) * HEAD_DIM
            halves.append(accs[half][half * HEAD_DIM:(half + 1) * HEAD_DIM] / accs[half][other:other + 1])
        o_ref[0, sub * nq:(sub + 1) * nq, c * LANES:(c + 1) * LANES] = (
            jnp.concatenate(halves, axis=0).T.astype(o_ref.dtype))

    units = [(sub, c) for c in range(NA_WIDTH // LANES) for sub in range(NA_SUBSTEPS)]
    ahead = NA_SUBSTEPS
    pending = [scores(*u) for u in units[:ahead]]
    for n, u in enumerate(units):
        if n + ahead < len(units):
            pending.append(scores(*units[n + ahead]))
        weighted(*u, *pending.pop(0))


def _na_attention(proj3, bias):
    nb, seq, _ = proj3.shape
    grid_rows = seq // GRID_W
    assert grid_rows % (NA_ROWS * NA_SUBSTEPS) == 0 and grid_rows >= NA_BAND
    steps = grid_rows // (NA_ROWS * NA_SUBSTEPS)
    nq = NA_ROWS * GRID_W
    return pl.pallas_call(
        functools.partial(_na_kernel, grid_rows=grid_rows),
        grid=(nb, steps),
        in_specs=[
            pl.BlockSpec((1, NA_SUBSTEPS * nq, NA_WIDTH), lambda b, g: (b, g, 0)),
            pl.BlockSpec((1, seq, NA_WIDTH), lambda b, g: (b, 0, 1)),
            pl.BlockSpec((1, seq, NA_WIDTH), lambda b, g: (b, 0, 2)),
            pl.BlockSpec(bias.shape, lambda b, g: (0, 0, 0, 0), pipeline_mode=pl.Buffered(1)),
        ],
        out_specs=pl.BlockSpec((1, NA_SUBSTEPS * nq, NA_WIDTH), lambda b, g: (b, g, 0)),
        out_shape=jax.ShapeDtypeStruct((nb, seq, NA_WIDTH), jnp.bfloat16),
        scratch_shapes=[pltpu.VMEM((2, seq // NA_KEY_TILE, NA_WIDTH, NA_KEY_TILE), jnp.bfloat16)],
        compiler_params=pltpu.CompilerParams(dimension_semantics=("parallel", "arbitrary"),
                                             vmem_limit_bytes=VMEM_LIMIT),
        name="neighbourhood_attention",
    )(proj3, proj3, proj3, bias)


def _na_bias_tables(rpb_l):
    i = np.arange(NA_ROWS)[:, None]
    a = np.arange(NA_BAND)[None, :]
    d = a - i
    start = np.stack([np.zeros_like(d), np.zeros_like(d) + i, np.zeros_like(d) + 4])
    row_ok = (a[None] >= start) & (a[None] < start + NA_WIN_H)
    row_off = np.stack([d + 7, d + 3, d - 1])
    row_off = np.clip(row_off, 0, 2 * NA_WIN_H - 2)
    cq = np.arange(GRID_W)[:, None]
    ck = np.arange(GRID_W)[None, :]
    c0 = np.clip(cq - NA_WIN_W // 2, 0, GRID_W - NA_WIN_W)
    col_ok = (ck >= c0) & (ck < c0 + NA_WIN_W)
    col_off = np.clip(ck - cq + NA_WIN_W - 1, 0, 2 * NA_WIN_W - 2)
    cols = jnp.where(col_ok.T[None, None], rpb_l.astype(jnp.float32)[:, :, col_off.T] * LOG2_E, MASKED)
    t = jnp.stack([cols[:, r] for r in row_off.reshape(-1)], axis=1)
    t = t.reshape(NA_HEADS // 2, 2, 3, NA_ROWS, NA_BAND, GRID_W, GRID_W)
    t = jnp.where(row_ok[None, None, :, :, :, None, None], t, MASKED)
    t = jnp.transpose(t, (2, 0, 4, 5, 1, 3, 6))
    return t.reshape(3, NA_HEADS // 2, NA_BAND * GRID_W, 2 * NA_ROWS * GRID_W)


def _gqa_kernel(trips_ref, q_ref, k_ref, v_ref, o_ref, vt_ref, s0_ref, s1_ref, acc0_ref, acc1_ref,
                qm0_ref, qm1_ref, stage_ref, *, seq):
    tq, tk = _gqa_q_tile(seq), GQA_K_TILE
    n_chunks = seq // tk
    s_refs = (s0_ref, s1_ref)
    acc_refs = (acc0_ref, acc1_ref)
    qm_refs = (qm0_ref, qm1_ref)
    MAX_ROWS = 8

    @pl.when(pl.program_id(1) == 0)
    def _():
        ones = jnp.ones((HEAD_DIM, tk), jnp.bfloat16)
        for j in range(n_chunks):
            t = v_ref[0, j * tk:(j + 1) * tk, :].astype(jnp.float32).T
            for hk in range(GQA_KV_HEADS):
                vt_ref[hk, j, :HEAD_DIM, :] = t[hk * HEAD_DIM:(hk + 1) * HEAD_DIM].astype(jnp.bfloat16)
                vt_ref[hk, j, HEAD_DIM:, :] = ones

    low = _low_half((tq, LANES))
    heads = [(c, hk) for c in range(GQA_Q_WIDTH // LANES) for hk in range(GQA_KV_HEADS)]

    def masked_q(c, hk):
        qc = q_ref[0, :, c * LANES:(c + 1) * LANES].astype(jnp.float32)
        own = low if hk == 0 else jnp.logical_not(low)
        return jnp.where(own, qc, 0.0).astype(jnp.bfloat16)

    def scores(j, qm, slot, mx):
        r0 = j * tk
        s = lax.dot_general(k_ref[0, pl.ds(r0, tk), :], qm, (((1,), (1,)), ((), ())),
                            preferred_element_type=jnp.float32)
        s_refs[slot][pl.ds(r0, tk), :] = s
        return jnp.maximum(mx, jnp.max(s.reshape(tk // MAX_ROWS, MAX_ROWS, tq), axis=0))

    def weighted(j, h, m):
        r0 = j * tk
        p = jnp.exp2(s_refs[h % 2][pl.ds(r0, tk), :] - m).astype(jnp.bfloat16)
        part = jnp.dot(vt_ref[heads[h][1], j], p, preferred_element_type=jnp.float32)
        if j == 0:
            acc_refs[h % 2][...] = part
        else:
            acc_refs[h % 2][...] += part

    def finish(h):
        c, hk = heads[h]
        acc = acc_refs[h % 2][...]
        stage_ref[hk * HEAD_DIM:(hk + 1) * HEAD_DIM, :] = acc[:HEAD_DIM] / acc[HEAD_DIM:HEAD_DIM + 1]
        if hk == GQA_KV_HEADS - 1:
            o_ref[0, :, c * LANES:(c + 1) * LANES] = stage_ref[...].T.astype(o_ref.dtype)

    n_heads = len(heads)
    mx0 = jnp.full((MAX_ROWS, tq), -jnp.inf, jnp.float32)
    qm_refs[0][...] = masked_q(*heads[0])
    m = None
    for i in range(n_heads + 1):
        def body(mx, i=i, m=m):
            if i >= 2:
                finish(i - 2)
            if i + 1 < n_heads:
                qm_refs[(i + 1) % 2][...] = masked_q(*heads[i + 1])
            qm = qm_refs[i % 2][...] if i < n_heads else None
            for j in range(n_chunks):
                if i >= 1:
                    weighted(j, i - 1, m)
                if i < n_heads:
                    mx = scores(j, qm, i % 2, mx)
            return mx

        mx = lax.fori_loop(0, trips_ref[0], lambda t, mx, body=body: body(mx), mx0)
        m = jnp.max(mx, axis=0, keepdims=True)
    finish(n_heads - 1)


def _gqa_q_tile(seq):
    return min(seq, GQA_MAX_Q_TILE, GQA_SCORE_BYTES // (2 * 4 * seq))


def _gqa_attention(proj3):
    nb, seq, _ = proj3.shape
    tq, tk = _gqa_q_tile(seq), GQA_K_TILE
    assert seq % tk == 0 and seq // tk <= GQA_MAX_CHUNKS and seq % tq == 0
    q_blk = 0
    k_blk = GQA_Q_WIDTH // GQA_KV_WIDTH
    trips = jnp.ones((1,), jnp.int32)
    return pl.pallas_call(
        functools.partial(_gqa_kernel, seq=seq),
        grid=(nb, seq // tq),
        in_specs=[
            pl.BlockSpec(memory_space=pltpu.SMEM),
            pl.BlockSpec((1, tq, GQA_Q_WIDTH), lambda b, i: (b, i, q_blk)),
            pl.BlockSpec((1, seq, GQA_KV_WIDTH), lambda b, i: (b, 0, k_blk)),
            pl.BlockSpec((1, seq, GQA_KV_WIDTH), lambda b, i: (b, 0, k_blk + 1)),
        ],
        out_specs=pl.BlockSpec((1, tq, GQA_Q_WIDTH), lambda b, i: (b, i, 0)),
        out_shape=jax.ShapeDtypeStruct((nb, seq, GQA_Q_WIDTH), jnp.bfloat16),
        scratch_shapes=[
            pltpu.VMEM((GQA_KV_HEADS, seq // tk, LANES, tk), jnp.bfloat16),
            pltpu.VMEM((seq, tq), jnp.float32),
            pltpu.VMEM((seq, tq), jnp.float32),
            pltpu.VMEM((LANES, tq), jnp.float32),
            pltpu.VMEM((LANES, tq), jnp.float32),
            pltpu.VMEM((tq, LANES), jnp.bfloat16),
            pltpu.VMEM((tq, LANES), jnp.bfloat16),
            pltpu.VMEM((LANES, tq), jnp.float32),
        ],
        compiler_params=pltpu.CompilerParams(dimension_semantics=("parallel", "arbitrary"),
                                             vmem_limit_bytes=VMEM_LIMIT),
        name="gqa_attention",
    )(trips, proj3, proj3, proj3)


def _sigmoid(x):
    return 1.0 / (1.0 + jnp.exp(-x))


def _out_proj_kernel(x_ref, oa_ref, ob_ref, g_ref, scale_ref, shift_ref, gate_ref, wzg_ref, bg_ref,
                     wpa_ref, wpb_ref, wout_ref, fg_ref, o_ref, *, final):
    sub = x_ref.shape[0] // TOKEN_SUBTILES
    for t in range(TOKEN_SUBTILES):
        rows = slice(t * sub, (t + 1) * sub)
        x = x_ref[rows, :]
        hid = _modulated_norm(x, g_ref[...], scale_ref[0], shift_ref[0])
        zg = jnp.dot(hid.astype(jnp.bfloat16), wzg_ref[...], preferred_element_type=jnp.float32)
        za = zg[:, :NA_WIDTH]
        zb = zg[:, NA_WIDTH:NA_WIDTH + GQA_Q_WIDTH]
        ua = oa_ref[rows, :].astype(jnp.float32) * (za * _sigmoid(za))
        ub = ob_ref[rows, :].astype(jnp.float32) * (zb * _sigmoid(zb))
        ya = jnp.dot(ua.astype(jnp.bfloat16), wpa_ref[...], preferred_element_type=jnp.float32)
        yb = jnp.dot(ub.astype(jnp.bfloat16), wpb_ref[...], preferred_element_type=jnp.float32)
        gates = _sigmoid(zg[:, NA_WIDTH + GQA_Q_WIDTH:] + bg_ref[...])
        merged = gates[:, :D_MODEL] * ya + gates[:, D_MODEL:] * yb
        y = jnp.dot(merged.astype(jnp.bfloat16), wout_ref[...], preferred_element_type=jnp.float32)
        out = x + gate_ref[0] * y
        if final:
            ms = jnp.mean(out * out, axis=-1, keepdims=True)
            out = out * lax.rsqrt(ms + EPS) * fg_ref[...]
        o_ref[rows, :] = out


def _out_projection(x2, oa, ob, norm_g, scale, shift, gate, wzg, b_gate, wpa, wpb, wout, final_g, seq, final):
    tokens = x2.shape[0]
    tm = TOKEN_TILE
    per_seq = seq // tm
    vec = lambda i: (0, 0)
    per_batch = lambda i: (i // per_seq, 0, 0)
    return pl.pallas_call(
        functools.partial(_out_proj_kernel, final=final),
        grid=(tokens // tm,),
        in_specs=[
            pl.BlockSpec((tm, D_MODEL), lambda i: (i, 0)),
            pl.BlockSpec((tm, NA_WIDTH), lambda i: (i, 0)),
            pl.BlockSpec((tm, GQA_Q_WIDTH), lambda i: (i, 0)),
            pl.BlockSpec((1, D_MODEL), vec),
            pl.BlockSpec((1, 1, D_MODEL), per_batch),
            pl.BlockSpec((1, 1, D_MODEL), per_batch),
            pl.BlockSpec((1, 1, D_MODEL), per_batch),
            pl.BlockSpec((D_MODEL, ZG_WIDTH), vec, pipeline_mode=pl.Buffered(1)),
            pl.BlockSpec((1, 2 * D_MODEL), vec),
            pl.BlockSpec((NA_WIDTH, D_MODEL), vec, pipeline_mode=pl.Buffered(1)),
            pl.BlockSpec((GQA_Q_WIDTH, D_MODEL), vec, pipeline_mode=pl.Buffered(1)),
            pl.BlockSpec((D_MODEL, D_MODEL), vec, pipeline_mode=pl.Buffered(1)),
            pl.BlockSpec((1, D_MODEL), vec),
        ],
        out_specs=pl.BlockSpec((tm, D_MODEL), lambda i: (i, 0)),
        out_shape=jax.ShapeDtypeStruct((tokens, D_MODEL), jnp.float32),
        compiler_params=pltpu.CompilerParams(dimension_semantics=("parallel",), vmem_limit_bytes=VMEM_LIMIT),
        name="out_projection",
    )(x2, oa, ob, norm_g, scale, shift, gate, wzg, b_gate, wpa, wpb, wout, final_g)


def _rope_tables(seq):
    t = jnp.arange(seq)
    row = (t // GRID_W).astype(jnp.float32)
    col = (t % GRID_W).astype(jnp.float32)
    inv = ROPE_THETA ** (-jnp.arange(0, ROPE_AXIS_DIM, 2, dtype=jnp.float32) / ROPE_AXIS_DIM)
    ang_r, ang_c = row[:, None] * inv, col[:, None] * inv
    cos = jnp.concatenate([jnp.cos(ang_r)] * 2 + [jnp.cos(ang_c)] * 2, axis=-1)
    sin = jnp.concatenate([-jnp.sin(ang_r), jnp.sin(ang_r), -jnp.sin(ang_c), jnp.sin(ang_c)], axis=-1)
    return jnp.tile(cos, (1, 2)), jnp.tile(sin, (1, 2))


def _prepare_layer(l, norm_g, w_in, b_gate, rpb, q_norm_g, k_norm_g, w_pa, w_pb, w_out):
    bounds = np.concatenate([[0], np.cumsum(IN_SPLITS)])
    qa, ka, va, za, qb, kb, vb, zb, ga, gb = (w_in[l][:, bounds[i]:bounds[i + 1]] for i in range(10))
    bf = jnp.bfloat16

    def pair_heads(w):
        r = w.shape[0]
        w = w.reshape(r, GQA_KV_HEADS, GQA_GROUP, HEAD_DIM)
        return jnp.transpose(w, (0, 2, 1, 3)).reshape(r, GQA_Q_WIDTH)

    return dict(
        norm_g=norm_g[l][None, :],
        w_gqa=jnp.concatenate([pair_heads(qb), kb, vb], axis=1).astype(bf),
        w_na=jnp.concatenate([qa * (ATTN_SCALE * LOG2_E), ka, va], axis=1).astype(bf),
        wzg=jnp.concatenate([za, pair_heads(zb), ga, gb], axis=1).astype(bf),
        b_gate=b_gate[l][None, :],
        bias=_na_bias_tables(rpb[l]),
        qg=jnp.tile(q_norm_g[l] * (ATTN_SCALE * LOG2_E), 2)[None, :],
        kg=jnp.tile(k_norm_g[l], 2)[None, :],
        wpa=w_pa[l].astype(bf),
        wpb=jnp.transpose(w_pb[l].reshape(GQA_KV_HEADS, GQA_GROUP, HEAD_DIM, D_MODEL),
                          (1, 0, 2, 3)).reshape(GQA_Q_WIDTH, D_MODEL).astype(bf),
        wout=w_out[l].astype(bf),
    )


def _trunk(x, mod, layers, final_g):
    nb, seq, _ = x.shape
    x2 = x.reshape(nb * seq, D_MODEL)
    cos_t, sin_t = _rope_tables(seq)
    for l, p in enumerate(layers):
        shift, scale, gate = (mod[l][:, None, i * D_MODEL:(i + 1) * D_MODEL] for i in range(3))
        proj_gqa, proj_na = _in_projection(x2, p["norm_g"], scale, shift, p["w_gqa"], p["w_na"], cos_t, sin_t,
                                           p["qg"], p["kg"], seq)
        oa = _na_attention(proj_na.reshape(nb, seq, NA_PROJ_WIDTH), p["bias"]).reshape(nb * seq, NA_WIDTH)
        ob = _gqa_attention(proj_gqa.reshape(nb, seq, GQA_PROJ_WIDTH)).reshape(nb * seq, GQA_Q_WIDTH)
        x2 = _out_projection(x2, oa, ob, p["norm_g"], scale, shift, gate, p["wzg"], p["b_gate"],
                             p["wpa"], p["wpb"], p["wout"], final_g[None, :], seq, final=(l == len(layers) - 1))
    return x2.reshape(nb, seq, D_MODEL)


def kernel(x_prompt, x_sample, c_prompt, c_sample, norm_g, w_ada, b_ada, w_in, b_gate, rpb, q_norm_g, k_norm_g, w_pa, w_pb, w_out, final_g):
    layers = [_prepare_layer(l, norm_g, w_in, b_gate, rpb, q_norm_g, k_norm_g, w_pa, w_pb, w_out)
              for l in range(DEPTH)]
    n_prompt = c_prompt.shape[0]
    mod = _modulation(jnp.concatenate([c_prompt, c_sample], axis=0), w_ada, b_ada)
    y_prompt = _trunk(x_prompt, mod[:, :n_prompt], layers, final_g)
    y_sample = _trunk(x_sample, mod[:, n_prompt:], layers, final_g)
    return (y_prompt, y_sample)
```

```python
import functools

import jax
import jax.numpy as jnp
import numpy as np
from jax import lax
from jax.experimental import pallas as pl
from jax.experimental.pallas import tpu as pltpu

D_MODEL = 1024
DEPTH = 4
GRID_W = 64
HEAD_DIM = 64
NA_HEADS = 8
NA_WIDTH = NA_HEADS * HEAD_DIM
NA_WIN_H = 8
NA_WIN_W = 16
GQA_Q_HEADS = 8
GQA_KV_HEADS = 2
GQA_GROUP = GQA_Q_HEADS // GQA_KV_HEADS
GQA_Q_WIDTH = GQA_Q_HEADS * HEAD_DIM
GQA_KV_WIDTH = GQA_KV_HEADS * HEAD_DIM
ROPE_AXIS_DIM = HEAD_DIM // 2
ROPE_THETA = 10000.0
EPS = 1e-6
ATTN_SCALE = HEAD_DIM ** -0.5
IN_SPLITS = (NA_WIDTH,) * 4 + (GQA_Q_WIDTH, GQA_KV_WIDTH, GQA_KV_WIDTH, GQA_Q_WIDTH, D_MODEL, D_MODEL)
MASKED = -1e30
LOG2_E = 1.4426950408889634

LANES = 128
NA_PROJ_WIDTH = 3 * NA_WIDTH
GQA_PROJ_WIDTH = GQA_Q_WIDTH + 2 * GQA_KV_WIDTH
ZG_WIDTH = NA_WIDTH + GQA_Q_WIDTH + 2 * D_MODEL
TOKEN_TILE = 1024
TOKEN_SUBTILES = 4
NA_ROWS = 4
NA_BAND = 12
NA_KEY_TILE = NA_ROWS * GRID_W
NA_SUBSTEPS = 4
GQA_SCORE_BYTES = 32 * 1024 * 1024
GQA_K_TILE = 256
GQA_MAX_Q_TILE = 1024
GQA_MAX_CHUNKS = 16
VMEM_LIMIT = 56 * 1024 * 1024


def _low_half(shape):
    return lax.broadcasted_iota(jnp.int32, shape, len(shape) - 1) < HEAD_DIM


def _mod_kernel(c_ref, w_ref, b_ref, o_ref):
    c = c_ref[...]
    sc = c * (1.0 / (1.0 + jnp.exp(-c)))
    o_ref[0] = jnp.dot(sc, w_ref[0], precision=lax.Precision.HIGHEST,
                       preferred_element_type=jnp.float32) + b_ref[0]


def _modulation(c_all, w_ada, b_ada):
    nb = c_all.shape[0]
    n_chunks = 3
    return pl.pallas_call(
        _mod_kernel,
        grid=(DEPTH, n_chunks),
        in_specs=[
            pl.BlockSpec((nb, D_MODEL), lambda l, j: (0, 0)),
            pl.BlockSpec((1, D_MODEL, D_MODEL), lambda l, j: (l, 0, j)),
            pl.BlockSpec((1, 1, D_MODEL), lambda l, j: (l, 0, j)),
        ],
        out_specs=pl.BlockSpec((1, nb, D_MODEL), lambda l, j: (l, 0, j)),
        out_shape=jax.ShapeDtypeStruct((DEPTH, nb, 3 * D_MODEL), jnp.float32),
        compiler_params=pltpu.CompilerParams(vmem_limit_bytes=VMEM_LIMIT),
        name="adaln_modulation",
    )(c_all, w_ada, b_ada.reshape(DEPTH, 1, 3 * D_MODEL))


def _modulated_norm(x, g, scale, shift):
    ms = jnp.mean(x * x, axis=-1, keepdims=True)
    y = x * lax.rsqrt(ms + EPS) * g
    return y * (1.0 + scale) + shift


def _head_norm_rope(x, g, cos, sin_signed, low):
    sq = x * x
    lo = jnp.sum(jnp.where(low, sq, 0.0), axis=-1, keepdims=True)
    hi = jnp.sum(jnp.where(low, 0.0, sq), axis=-1, keepdims=True)
    ms = jnp.where(low, lo, hi) * (1.0 / HEAD_DIM)
    y = x * lax.rsqrt(ms + EPS) * g
    lane = lax.broadcasted_iota(jnp.int32, x.shape, 1)
    quarter = ROPE_AXIS_DIM // 2
    partner = jnp.where((lane & quarter) != 0, pltpu.roll(y, quarter, 1), pltpu.roll(y, LANES - quarter, 1))
    return y * cos + partner * sin_signed


def _in_proj_kernel(x_ref, g_ref, scale_ref, shift_ref, wb_ref, wa_ref, cos_ref, sin_ref, qg_ref, kg_ref,
                    ob_ref, oa_ref):
    n_q_chunks = GQA_Q_WIDTH // LANES
    v0 = GQA_Q_WIDTH + GQA_KV_WIDTH
    sub = x_ref.shape[0] // TOKEN_SUBTILES
    low = _low_half((sub, LANES))
    for t in range(TOKEN_SUBTILES):
        rows = slice(t * sub, (t + 1) * sub)
        hid = _modulated_norm(x_ref[rows, :], g_ref[...], scale_ref[0], shift_ref[0]).astype(jnp.bfloat16)
        acc = jnp.dot(hid, wb_ref[...], preferred_element_type=jnp.float32)
        cos = cos_ref[rows, :]
        sin = sin_ref[rows, :]
        for j in range(n_q_chunks + 1):
            c0 = j * LANES
            gain = qg_ref[...] if j < n_q_chunks else kg_ref[...]
            ob_ref[rows, c0:c0 + LANES] = _head_norm_rope(acc[:, c0:c0 + LANES], gain, cos, sin,
                                                          low).astype(ob_ref.dtype)
        ob_ref[rows, v0:] = acc[:, v0:].astype(ob_ref.dtype)
        oa_ref[rows, :] = jnp.dot(hid, wa_ref[...], preferred_element_type=jnp.float32).astype(oa_ref.dtype)


def _in_projection(x2, norm_g, scale, shift, wb, wa, cos_t, sin_t, qg, kg, seq):
    tokens = x2.shape[0]
    tm = TOKEN_TILE
    per_seq = seq // tm
    vec = lambda i: (0, 0)
    return pl.pallas_call(
        _in_proj_kernel,
        grid=(tokens // tm,),
        in_specs=[
            pl.BlockSpec((tm, D_MODEL), lambda i: (i, 0)),
            pl.BlockSpec((1, D_MODEL), vec),
            pl.BlockSpec((1, 1, D_MODEL), lambda i: (i // per_seq, 0, 0)),
            pl.BlockSpec((1, 1, D_MODEL), lambda i: (i // per_seq, 0, 0)),
            pl.BlockSpec((D_MODEL, GQA_PROJ_WIDTH), vec, pipeline_mode=pl.Buffered(1)),
            pl.BlockSpec((D_MODEL, NA_PROJ_WIDTH), vec, pipeline_mode=pl.Buffered(1)),
            pl.BlockSpec((tm, LANES), lambda i: (i % per_seq, 0)),
            pl.BlockSpec((tm, LANES), lambda i: (i % per_seq, 0)),
            pl.BlockSpec((1, LANES), vec),
            pl.BlockSpec((1, LANES), vec),
        ],
        out_specs=[pl.BlockSpec((tm, GQA_PROJ_WIDTH), lambda i: (i, 0)),
                   pl.BlockSpec((tm, NA_PROJ_WIDTH), lambda i: (i, 0))],
        out_shape=[jax.ShapeDtypeStruct((tokens, GQA_PROJ_WIDTH), jnp.bfloat16),
                   jax.ShapeDtypeStruct((tokens, NA_PROJ_WIDTH), jnp.bfloat16)],
        compiler_params=pltpu.CompilerParams(dimension_semantics=("parallel",), vmem_limit_bytes=VMEM_LIMIT),
        name="in_projection",
    )(x2, norm_g, scale, shift, wb, wa, cos_t, sin_t, qg, kg)


def _na_kernel(q_ref, k_ref, v_ref, bias_ref, o_ref, vt_ref, *, grid_rows):
    step = pl.program_id(1)
    seq = grid_rows * GRID_W
    nk = NA_BAND * GRID_W
    nq = NA_ROWS * GRID_W
    kt = NA_KEY_TILE
    last_group = grid_rows // NA_ROWS - 1

    @pl.when(step == 0)
    def _():
        row = lax.broadcasted_iota(jnp.int32, (NA_WIDTH, kt), 0)
        even_rows = (row & HEAD_DIM) == 0
        for j in range(seq // kt):
            t = v_ref[0, j * kt:(j + 1) * kt, :].astype(jnp.float32).T
            vt_ref[0, j] = jnp.where(even_rows, t, 1.0).astype(jnp.bfloat16)
            vt_ref[1, j] = jnp.where(even_rows, 1.0, t).astype(jnp.bfloat16)

    low = _low_half((nq, LANES))
    groups = [NA_SUBSTEPS * step + sub for sub in range(NA_SUBSTEPS)]
    band_rows = [jnp.clip(NA_ROWS * g - NA_WIN_H // 2, 0, grid_rows - NA_BAND) for g in groups]
    variants = [(g > 0).astype(jnp.int32) + (g == last_group).astype(jnp.int32) for g in groups]

    def scores(sub, c):
        k0 = pl.multiple_of(band_rows[sub] * GRID_W, kt)
        qc = q_ref[0, sub * nq:(sub + 1) * nq, c * LANES:(c + 1) * LANES].astype(jnp.float32)
        qm = jnp.concatenate([jnp.where(low, qc, 0.0), jnp.where(low, 0.0, qc)], axis=0).astype(jnp.bfloat16)
        s = lax.dot_general(k_ref[0, pl.ds(k0, nk), c * LANES:(c + 1) * LANES], qm, (((1,), (1,)), ((), ())),
                            preferred_element_type=jnp.float32)
        s = s + bias_ref[variants[sub], c]
        return s, jnp.max(jnp.max(s.reshape(nk // 8, 8, 2 * nq), axis=0), axis=0, keepdims=True)

    def weighted(sub, c, s, m):
        kb = band_rows[sub] // (kt // GRID_W)
        accs = [jnp.zeros((LANES, nq), jnp.float32) for _ in range(2)]
        for t in range(nk // kt):
            p = jnp.exp2(s[t * kt:(t + 1) * kt] - m).astype(jnp.bfloat16)
            for half in range(2):
                accs[half] = accs[half] + jnp.dot(vt_ref[half, kb + t, c * LANES:(c + 1) * LANES, :],
                                                  p[:, half * nq:(half + 1) * nq],
                                                  preferred_element_type=jnp.float32)
        halves = []
        for half in range(2):
            other = (1 - half) * HEAD_DIM
            halves.append(accs[half][half * HEAD_DIM:(half + 1) * HEAD_DIM] / accs[half][other:other + 1])
        o_ref[0, sub * nq:(sub + 1) * nq, c * LANES:(c + 1) * LANES] = (
            jnp.concatenate(halves, axis=0).T.astype(o_ref.dtype))

    units = [(sub, c) for c in range(NA_WIDTH // LANES) for sub in range(NA_SUBSTEPS)]
    ahead = NA_SUBSTEPS
    pending = [scores(*u) for u in units[:ahead]]
    for n, u in enumerate(units):
        if n + ahead < len(units):
            pending.append(scores(*units[n + ahead]))
        weighted(*u, *pending.pop(0))


def _na_attention(proj3, bias):
    nb, seq, _ = proj3.shape
    grid_rows = seq // GRID_W
    assert grid_rows % (NA_ROWS * NA_SUBSTEPS) == 0 and grid_rows >= NA_BAND
    steps = grid_rows // (NA_ROWS * NA_SUBSTEPS)
    nq = NA_ROWS * GRID_W
    return pl.pallas_call(
        functools.partial(_na_kernel, grid_rows=grid_rows),
        grid=(nb, steps),
        in_specs=[
            pl.BlockSpec((1, NA_SUBSTEPS * nq, NA_WIDTH), lambda b, g: (b, g, 0)),
            pl.BlockSpec((1, seq, NA_WIDTH), lambda b, g: (b, 0, 1)),
            pl.BlockSpec((1, seq, NA_WIDTH), lambda b, g: (b, 0, 2)),
            pl.BlockSpec(bias.shape, lambda b, g: (0, 0, 0, 0), pipeline_mode=pl.Buffered(1)),
        ],
        out_specs=pl.BlockSpec((1, NA_SUBSTEPS * nq, NA_WIDTH), lambda b, g: (b, g, 0)),
        out_shape=jax.ShapeDtypeStruct((nb, seq, NA_WIDTH), jnp.bfloat16),
        scratch_shapes=[pltpu.VMEM((2, seq // NA_KEY_TILE, NA_WIDTH, NA_KEY_TILE), jnp.bfloat16)],
        compiler_params=pltpu.CompilerParams(dimension_semantics=("parallel", "arbitrary"),
                                             vmem_limit_bytes=VMEM_LIMIT),
        name="neighbourhood_attention",
    )(proj3, proj3, proj3, bias)


def _na_bias_tables(rpb_l):
    i = np.arange(NA_ROWS)[:, None]
    a = np.arange(NA_BAND)[None, :]
    d = a - i
    start = np.stack([np.zeros_like(d), np.zeros_like(d) + i, np.zeros_like(d) + 4])
    row_ok = (a[None] >= start) & (a[None] < start + NA_WIN_H)
    row_off = np.stack([d + 7, d + 3, d - 1])
    row_off = np.clip(row_off, 0, 2 * NA_WIN_H - 2)
    cq = np.arange(GRID_W)[:, None]
    ck = np.arange(GRID_W)[None, :]
    c0 = np.clip(cq - NA_WIN_W // 2, 0, GRID_W - NA_WIN_W)
    col_ok = (ck >= c0) & (ck < c0 + NA_WIN_W)
    pad = GRID_W - NA_WIN_W
    padded = jnp.pad(rpb_l.astype(jnp.float32) * LOG2_E, ((0, 0), (0, 0), (pad, pad)))
    cols = jnp.stack([padded[:, :, GRID_W - 1 - q:2 * GRID_W - 1 - q] for q in range(GRID_W)], axis=-1)
    cols = jnp.where(col_ok.T[None, None], cols, MASKED)
    cols = cols.reshape(NA_HEADS // 2, 2, 2 * NA_WIN_H - 1, GRID_W, GRID_W)
    masked_block = jnp.full((NA_HEADS // 2, GRID_W, GRID_W), MASKED, jnp.float32)
    pieces = []
    for hp in range(2):
        for qi in range(NA_ROWS):
            pieces.append(jnp.stack([
                jnp.stack([cols[:, hp, row_off[v, qi, ai]] if row_ok[v, qi, ai] else masked_block
                           for ai in range(NA_BAND)], axis=1)
                for v in range(3)], axis=0))
    t = jnp.concatenate(pieces, axis=-1)
    return t.reshape(3, NA_HEADS // 2, NA_BAND * GRID_W, 2 * NA_ROWS * GRID_W)


def _gqa_kernel(trips_ref, q_ref, k_ref, v_ref, o_ref, vt_ref, s0_ref, s1_ref, acc0_ref, acc1_ref,
                qm0_ref, qm1_ref, stage_ref, *, seq):
    tq, tk = _gqa_q_tile(seq), GQA_K_TILE
    n_chunks = seq // tk
    s_refs = (s0_ref, s1_ref)
    acc_refs = (acc0_ref, acc1_ref)
    qm_refs = (qm0_ref, qm1_ref)
    MAX_ROWS = 8

    @pl.when(pl.program_id(1) == 0)
    def _():
        ones = jnp.ones((HEAD_DIM, tk), jnp.bfloat16)
        for j in range(n_chunks):
            t = v_ref[0, j * tk:(j + 1) * tk, :].astype(jnp.float32).T
            for hk in range(GQA_KV_HEADS):
                vt_ref[hk, j, :HEAD_DIM, :] = t[hk * HEAD_DIM:(hk + 1) * HEAD_DIM].astype(jnp.bfloat16)
                vt_ref[hk, j, HEAD_DIM:, :] = ones

    low = _low_half((tq, LANES))
    heads = [(c, hk) for c in range(GQA_Q_WIDTH // LANES) for hk in range(GQA_KV_HEADS)]

    def masked_q(c, hk):
        qc = q_ref[0, :, c * LANES:(c + 1) * LANES].astype(jnp.float32)
        own = low if hk == 0 else jnp.logical_not(low)
        return jnp.where(own, qc, 0.0).astype(jnp.bfloat16)

    def scores(j, qm, slot, mx):
        r0 = j * tk
        s = lax.dot_general(k_ref[0, pl.ds(r0, tk), :], qm, (((1,), (1,)), ((), ())),
                            preferred_element_type=jnp.float32)
        s_refs[slot][pl.ds(r0, tk), :] = s
        return jnp.maximum(mx, jnp.max(s.reshape(tk // MAX_ROWS, MAX_ROWS, tq), axis=0))

    def weighted(j, h, m):
        r0 = j * tk
        p = jnp.exp2(s_refs[h % 2][pl.ds(r0, tk), :] - m).astype(jnp.bfloat16)
        part = jnp.dot(vt_ref[heads[h][1], j], p, preferred_element_type=jnp.float32)
        if j == 0:
            acc_refs[h % 2][...] = part
        else:
            acc_refs[h % 2][...] += part

    def finish(h):
        c, hk = heads[h]
        acc = acc_refs[h % 2][...]
        stage_ref[hk * HEAD_DIM:(hk + 1) * HEAD_DIM, :] = acc[:HEAD_DIM] / acc[HEAD_DIM:HEAD_DIM + 1]
        if hk == GQA_KV_HEADS - 1:
            o_ref[0, :, c * LANES:(c + 1) * LANES] = stage_ref[...].T.astype(o_ref.dtype)

    n_heads = len(heads)
    mx0 = jnp.full((MAX_ROWS, tq), -jnp.inf, jnp.float32)
    qm_refs[0][...] = masked_q(*heads[0])
    m = None
    for i in range(n_heads + 1):
        def body(mx, i=i, m=m):
            if i >= 2:
                finish(i - 2)
            if i + 1 < n_heads:
                qm_refs[(i + 1) % 2][...] = masked_q(*heads[i + 1])
            qm = qm_refs[i % 2][...] if i < n_heads else None
            for j in range(n_chunks):
                if i >= 1:
                    weighted(j, i - 1, m)
                if i < n_heads:
                    mx = scores(j, qm, i % 2, mx)
            return mx

        mx = lax.fori_loop(0, trips_ref[0], lambda t, mx, body=body: body(mx), mx0)
        m = jnp.max(mx, axis=0, keepdims=True)
    finish(n_heads - 1)


def _gqa_q_tile(seq):
    return min(seq, GQA_MAX_Q_TILE, GQA_SCORE_BYTES // (2 * 4 * seq))


def _gqa_attention(proj3):
    nb, seq, _ = proj3.shape
    tq, tk = _gqa_q_tile(seq), GQA_K_TILE
    assert seq % tk == 0 and seq // tk <= GQA_MAX_CHUNKS and seq % tq == 0
    q_blk = 0
    k_blk = GQA_Q_WIDTH // GQA_KV_WIDTH
    trips = jnp.ones((1,), jnp.int32)
    return pl.pallas_call(
        functools.partial(_gqa_kernel, seq=seq),
        grid=(nb, seq // tq),
        in_specs=[
            pl.BlockSpec(memory_space=pltpu.SMEM),
            pl.BlockSpec((1, tq, GQA_Q_WIDTH), lambda b, i: (b, i, q_blk)),
            pl.BlockSpec((1, seq, GQA_KV_WIDTH), lambda b, i: (b, 0, k_blk)),
            pl.BlockSpec((1, seq, GQA_KV_WIDTH), lambda b, i: (b, 0, k_blk + 1)),
        ],
        out_specs=pl.BlockSpec((1, tq, GQA_Q_WIDTH), lambda b, i: (b, i, 0)),
        out_shape=jax.ShapeDtypeStruct((nb, seq, GQA_Q_WIDTH), jnp.bfloat16),
        scratch_shapes=[
            pltpu.VMEM((GQA_KV_HEADS, seq // tk, LANES, tk), jnp.bfloat16),
            pltpu.VMEM((seq, tq), jnp.float32),
            pltpu.VMEM((seq, tq), jnp.float32),
            pltpu.VMEM((LANES, tq), jnp.float32),
            pltpu.VMEM((LANES, tq), jnp.float32),
            pltpu.VMEM((tq, LANES), jnp.bfloat16),
            pltpu.VMEM((tq, LANES), jnp.bfloat16),
            pltpu.VMEM((LANES, tq), jnp.float32),
        ],
        compiler_params=pltpu.CompilerParams(dimension_semantics=("parallel", "arbitrary"),
                                             vmem_limit_bytes=VMEM_LIMIT),
        name="gqa_attention",
    )(trips, proj3, proj3, proj3)


def _sigmoid(x):
    return 1.0 / (1.0 + jnp.exp(-x))


def _out_proj_kernel(x_ref, oa_ref, ob_ref, g_ref, scale_ref, shift_ref, gate_ref, wzg_ref, bg_ref,
                     wpa_ref, wpb_ref, wout_ref, fg_ref, o_ref, *, final):
    sub = x_ref.shape[0] // TOKEN_SUBTILES
    for t in range(TOKEN_SUBTILES):
        rows = slice(t * sub, (t + 1) * sub)
        x = x_ref[rows, :]
        hid = _modulated_norm(x, g_ref[...], scale_ref[0], shift_ref[0])
        zg = jnp.dot(hid.astype(jnp.bfloat16), wzg_ref[...], preferred_element_type=jnp.float32)
        za = zg[:, :NA_WIDTH]
        zb = zg[:, NA_WIDTH:NA_WIDTH + GQA_Q_WIDTH]
        ua = oa_ref[rows, :].astype(jnp.float32) * (za * _sigmoid(za))
        ub = ob_ref[rows, :].astype(jnp.float32) * (zb * _sigmoid(zb))
        ya = jnp.dot(ua.astype(jnp.bfloat16), wpa_ref[...], preferred_element_type=jnp.float32)
        yb = jnp.dot(ub.astype(jnp.bfloat16), wpb_ref[...], preferred_element_type=jnp.float32)
        gates = _sigmoid(zg[:, NA_WIDTH + GQA_Q_WIDTH:] + bg_ref[...])
        merged = gates[:, :D_MODEL] * ya + gates[:, D_MODEL:] * yb
        y = jnp.dot(merged.astype(jnp.bfloat16), wout_ref[...], preferred_element_type=jnp.float32)
        out = x + gate_ref[0] * y
        if final:
            ms = jnp.mean(out * out, axis=-1, keepdims=True)
            out = out * lax.rsqrt(ms + EPS) * fg_ref[...]
        o_ref[rows, :] = out


def _out_projection(x2, oa, ob, norm_g, scale, shift, gate, wzg, b_gate, wpa, wpb, wout, final_g, seq, final):
    tokens = x2.shape[0]
    tm = TOKEN_TILE
    per_seq = seq // tm
    vec = lambda i: (0, 0)
    per_batch = lambda i: (i // per_seq, 0, 0)
    return pl.pallas_call(
        functools.partial(_out_proj_kernel, final=final),
        grid=(tokens // tm,),
        in_specs=[
            pl.BlockSpec((tm, D_MODEL), lambda i: (i, 0)),
            pl.BlockSpec((tm, NA_WIDTH), lambda i: (i, 0)),
            pl.BlockSpec((tm, GQA_Q_WIDTH), lambda i: (i, 0)),
            pl.BlockSpec((1, D_MODEL), vec),
            pl.BlockSpec((1, 1, D_MODEL), per_batch),
            pl.BlockSpec((1, 1, D_MODEL), per_batch),
            pl.BlockSpec((1, 1, D_MODEL), per_batch),
            pl.BlockSpec((D_MODEL, ZG_WIDTH), vec, pipeline_mode=pl.Buffered(1)),
            pl.BlockSpec((1, 2 * D_MODEL), vec),
            pl.BlockSpec((NA_WIDTH, D_MODEL), vec, pipeline_mode=pl.Buffered(1)),
            pl.BlockSpec((GQA_Q_WIDTH, D_MODEL), vec, pipeline_mode=pl.Buffered(1)),
            pl.BlockSpec((D_MODEL, D_MODEL), vec, pipeline_mode=pl.Buffered(1)),
            pl.BlockSpec((1, D_MODEL), vec),
        ],
        out_specs=pl.BlockSpec((tm, D_MODEL), lambda i: (i, 0)),
        out_shape=jax.ShapeDtypeStruct((tokens, D_MODEL), jnp.float32),
        compiler_params=pltpu.CompilerParams(dimension_semantics=("parallel",), vmem_limit_bytes=VMEM_LIMIT),
        name="out_projection",
    )(x2, oa, ob, norm_g, scale, shift, gate, wzg, b_gate, wpa, wpb, wout, final_g)


def _rope_tables(seq):
    t = jnp.arange(seq)
    row = (t // GRID_W).astype(jnp.float32)
    col = (t % GRID_W).astype(jnp.float32)
    inv = ROPE_THETA ** (-jnp.arange(0, ROPE_AXIS_DIM, 2, dtype=jnp.float32) / ROPE_AXIS_DIM)
    ang_r, ang_c = row[:, None] * inv, col[:, None] * inv
    cos = jnp.concatenate([jnp.cos(ang_r)] * 2 + [jnp.cos(ang_c)] * 2, axis=-1)
    sin = jnp.concatenate([-jnp.sin(ang_r), jnp.sin(ang_r), -jnp.sin(ang_c), jnp.sin(ang_c)], axis=-1)
    return jnp.tile(cos, (1, 2)), jnp.tile(sin, (1, 2))


def _prepare_layer(l, norm_g, w_in, b_gate, rpb, q_norm_g, k_norm_g, w_pa, w_pb, w_out):
    bounds = np.concatenate([[0], np.cumsum(IN_SPLITS)])
    qa, ka, va, za, qb, kb, vb, zb, ga, gb = (w_in[l][:, bounds[i]:bounds[i + 1]] for i in range(10))
    bf = jnp.bfloat16

    def pair_heads(w):
        r = w.shape[0]
        w = w.reshape(r, GQA_KV_HEADS, GQA_GROUP, HEAD_DIM)
        return jnp.transpose(w, (0, 2, 1, 3)).reshape(r, GQA_Q_WIDTH)

    return dict(
        norm_g=norm_g[l][None, :],
        w_gqa=jnp.concatenate([pair_heads(qb), kb, vb], axis=1).astype(bf),
        w_na=jnp.concatenate([qa * (ATTN_SCALE * LOG2_E), ka, va], axis=1).astype(bf),
        wzg=jnp.concatenate([za, pair_heads(zb), ga, gb], axis=1).astype(bf),
        b_gate=b_gate[l][None, :],
        bias=_na_bias_tables(rpb[l]),
        qg=jnp.tile(q_norm_g[l] * (ATTN_SCALE * LOG2_E), 2)[None, :],
        kg=jnp.tile(k_norm_g[l], 2)[None, :],
        wpa=w_pa[l].astype(bf),
        wpb=jnp.transpose(w_pb[l].reshape(GQA_KV_HEADS, GQA_GROUP, HEAD_DIM, D_MODEL),
                          (1, 0, 2, 3)).reshape(GQA_Q_WIDTH, D_MODEL).astype(bf),
        wout=w_out[l].astype(bf),
    )


def _trunk(x, mod, layers, final_g):
    nb, seq, _ = x.shape
    x2 = x.reshape(nb * seq, D_MODEL)
    cos_t, sin_t = _rope_tables(seq)
    for l, p in enumerate(layers):
        shift, scale, gate = (mod[l][:, None, i * D_MODEL:(i + 1) * D_MODEL] for i in range(3))
        proj_gqa, proj_na = _in_projection(x2, p["norm_g"], scale, shift, p["w_gqa"], p["w_na"], cos_t, sin_t,
                                           p["qg"], p["kg"], seq)
        oa = _na_attention(proj_na.reshape(nb, seq, NA_PROJ_WIDTH), p["bias"]).reshape(nb * seq, NA_WIDTH)
        ob = _gqa_attention(proj_gqa.reshape(nb, seq, GQA_PROJ_WIDTH)).reshape(nb * seq, GQA_Q_WIDTH)
        x2 = _out_projection(x2, oa, ob, p["norm_g"], scale, shift, gate, p["wzg"], p["b_gate"],
                             p["wpa"], p["wpb"], p["wout"], final_g[None, :], seq, final=(l == len(layers) - 1))
    return x2.reshape(nb, seq, D_MODEL)


def kernel(x_prompt, x_sample, c_prompt, c_sample, norm_g, w_ada, b_ada, w_in, b_gate, rpb, q_norm_g, k_norm_g, w_pa, w_pb, w_out, final_g):
    layers = [_prepare_layer(l, norm_g, w_in, b_gate, rpb, q_norm_g, k_norm_g, w_pa, w_pb, w_out)
              for l in range(DEPTH)]
    n_prompt = c_prompt.shape[0]
    mod = _modulation(jnp.concatenate([c_prompt, c_sample], axis=0), w_ada, b_ada)
    y_prompt = _trunk(x_prompt, mod[:, :n_prompt], layers, final_g)
    y_sample = _trunk(x_sample, mod[:, n_prompt:], layers, final_g)
    return (y_prompt, y_sample)
```

```python
import functools

import jax
import jax.numpy as jnp
import numpy as np
from jax import lax
from jax.experimental import pallas as pl
from jax.experimental.pallas import tpu as pltpu

D_MODEL = 1024
DEPTH = 4
GRID_W = 64
HEAD_DIM = 64
NA_HEADS = 8
NA_WIDTH = NA_HEADS * HEAD_DIM
NA_WIN_H = 8
NA_WIN_W = 16
GQA_Q_HEADS = 8
GQA_KV_HEADS = 2
GQA_GROUP = GQA_Q_HEADS // GQA_KV_HEADS
GQA_Q_WIDTH = GQA_Q_HEADS * HEAD_DIM
GQA_KV_WIDTH = GQA_KV_HEADS * HEAD_DIM
ROPE_AXIS_DIM = HEAD_DIM // 2
ROPE_THETA = 10000.0
EPS = 1e-6
ATTN_SCALE = HEAD_DIM ** -0.5
IN_SPLITS = (NA_WIDTH,) * 4 + (GQA_Q_WIDTH, GQA_KV_WIDTH, GQA_KV_WIDTH, GQA_Q_WIDTH, D_MODEL, D_MODEL)
MASKED = -1e30
LOG2_E = 1.4426950408889634

LANES = 128
NA_PROJ_WIDTH = 3 * NA_WIDTH
GQA_PROJ_WIDTH = GQA_Q_WIDTH + 2 * GQA_KV_WIDTH
ZG_WIDTH = NA_WIDTH + GQA_Q_WIDTH + 2 * D_MODEL
TOKEN_TILE = 1024
TOKEN_SUBTILES = 4
NA_ROWS = 4
NA_BAND = 12
NA_KEY_TILE = NA_ROWS * GRID_W
NA_SUBSTEPS = 4
GQA_SCORE_BYTES = 32 * 1024 * 1024
GQA_K_TILE = 256
GQA_MAX_Q_TILE = 1024
GQA_MAX_CHUNKS = 16
VMEM_LIMIT = 56 * 1024 * 1024


def _low_half(shape):
    return lax.broadcasted_iota(jnp.int32, shape, len(shape) - 1) < HEAD_DIM


def _mod_kernel(c_ref, w_ref, b_ref, o_ref):
    c = c_ref[...]
    sc = c * (1.0 / (1.0 + jnp.exp(-c)))
    o_ref[0] = jnp.dot(sc, w_ref[0], precision=lax.Precision.HIGHEST,
                       preferred_element_type=jnp.float32) + b_ref[0]


def _modulation(c_all, w_ada, b_ada):
    nb = c_all.shape[0]
    n_chunks = 3
    return pl.pallas_call(
        _mod_kernel,
        grid=(DEPTH, n_chunks),
        in_specs=[
            pl.BlockSpec((nb, D_MODEL), lambda l, j: (0, 0)),
            pl.BlockSpec((1, D_MODEL, D_MODEL), lambda l, j: (l, 0, j)),
            pl.BlockSpec((1, 1, D_MODEL), lambda l, j: (l, 0, j)),
        ],
        out_specs=pl.BlockSpec((1, nb, D_MODEL), lambda l, j: (l, 0, j)),
        out_shape=jax.ShapeDtypeStruct((DEPTH, nb, 3 * D_MODEL), jnp.float32),
        compiler_params=pltpu.CompilerParams(vmem_limit_bytes=VMEM_LIMIT),
        name="adaln_modulation",
    )(c_all, w_ada, b_ada.reshape(DEPTH, 1, 3 * D_MODEL))


def _modulated_norm(x, g, scale, shift):
    ms = jnp.mean(x * x, axis=-1, keepdims=True)
    y = x * lax.rsqrt(ms + EPS) * g
    return y * (1.0 + scale) + shift


def _head_norm_rope(x, g, cos, sin_signed, low):
    sq = x * x
    lo = jnp.sum(jnp.where(low, sq, 0.0), axis=-1, keepdims=True)
    hi = jnp.sum(jnp.where(low, 0.0, sq), axis=-1, keepdims=True)
    ms = jnp.where(low, lo, hi) * (1.0 / HEAD_DIM)
    y = x * lax.rsqrt(ms + EPS) * g
    lane = lax.broadcasted_iota(jnp.int32, x.shape, 1)
    quarter = ROPE_AXIS_DIM // 2
    partner = jnp.where((lane & quarter) != 0, pltpu.roll(y, quarter, 1), pltpu.roll(y, LANES - quarter, 1))
    return y * cos + partner * sin_signed


def _in_proj_kernel(x_ref, g_ref, scale_ref, shift_ref, wb_ref, wa_ref, cos_ref, sin_ref, qg_ref, kg_ref,
                    ob_ref, oa_ref):
    n_q_chunks = GQA_Q_WIDTH // LANES
    v0 = GQA_Q_WIDTH + GQA_KV_WIDTH
    sub = x_ref.shape[0] // TOKEN_SUBTILES
    low = _low_half((sub, LANES))
    for t in range(TOKEN_SUBTILES):
        rows = slice(t * sub, (t + 1) * sub)
        hid = _modulated_norm(x_ref[rows, :], g_ref[...], scale_ref[0], shift_ref[0]).astype(jnp.bfloat16)
        acc = jnp.dot(hid, wb_ref[...], preferred_element_type=jnp.float32)
        cos = cos_ref[rows, :]
        sin = sin_ref[rows, :]
        for j in range(n_q_chunks + 1):
            c0 = j * LANES
            gain = qg_ref[...] if j < n_q_chunks else kg_ref[...]
            ob_ref[rows, c0:c0 + LANES] = _head_norm_rope(acc[:, c0:c0 + LANES], gain, cos, sin,
                                                          low).astype(ob_ref.dtype)
        ob_ref[rows, v0:] = acc[:, v0:].astype(ob_ref.dtype)
        oa_ref[rows, :] = jnp.dot(hid, wa_ref[...], preferred_element_type=jnp.float32).astype(oa_ref.dtype)


def _in_projection(x2, norm_g, scale, shift, wb, wa, cos_t, sin_t, qg, kg, seq):
    tokens = x2.shape[0]
    tm = TOKEN_TILE
    per_seq = seq // tm
    vec = lambda i: (0, 0)
    return pl.pallas_call(
        _in_proj_kernel,
        grid=(tokens // tm,),
        in_specs=[
            pl.BlockSpec((tm, D_MODEL), lambda i: (i, 0)),
            pl.BlockSpec((1, D_MODEL), vec),
            pl.BlockSpec((1, 1, D_MODEL), lambda i: (i // per_seq, 0, 0)),
            pl.BlockSpec((1, 1, D_MODEL), lambda i: (i // per_seq, 0, 0)),
            pl.BlockSpec((D_MODEL, GQA_PROJ_WIDTH), vec, pipeline_mode=pl.Buffered(1)),
            pl.BlockSpec((D_MODEL, NA_PROJ_WIDTH), vec, pipeline_mode=pl.Buffered(1)),
            pl.BlockSpec((tm, LANES), lambda i: (i % per_seq, 0)),
            pl.BlockSpec((tm, LANES), lambda i: (i % per_seq, 0)),
            pl.BlockSpec((1, LANES), vec),
            pl.BlockSpec((1, LANES), vec),
        ],
        out_specs=[pl.BlockSpec((tm, GQA_PROJ_WIDTH), lambda i: (i, 0)),
                   pl.BlockSpec((tm, NA_PROJ_WIDTH), lambda i: (i, 0))],
        out_shape=[jax.ShapeDtypeStruct((tokens, GQA_PROJ_WIDTH), jnp.bfloat16),
                   jax.ShapeDtypeStruct((tokens, NA_PROJ_WIDTH), jnp.bfloat16)],
        compiler_params=pltpu.CompilerParams(dimension_semantics=("parallel",), vmem_limit_bytes=VMEM_LIMIT),
        name="in_projection",
    )(x2, norm_g, scale, shift, wb, wa, cos_t, sin_t, qg, kg)


def _na_kernel(q_ref, k_ref, v_ref, bias_ref, o_ref, vt_ref, *, grid_rows):
    step = pl.program_id(1)
    seq = grid_rows * GRID_W
    nk = NA_BAND * GRID_W
    nq = NA_ROWS * GRID_W
    kt = NA_KEY_TILE
    last_group = grid_rows // NA_ROWS - 1

    @pl.when(step == 0)
    def _():
        row = lax.broadcasted_iota(jnp.int32, (NA_WIDTH, kt), 0)
        even_rows = (row & HEAD_DIM) == 0
        for j in range(seq // kt):
            t = v_ref[0, j * kt:(j + 1) * kt, :].astype(jnp.float32).T
            vt_ref[0, j] = jnp.where(even_rows, t, 1.0).astype(jnp.bfloat16)
            vt_ref[1, j] = jnp.where(even_rows, 1.0, t).astype(jnp.bfloat16)

    low = _low_half((nq, LANES))
    groups = [NA_SUBSTEPS * step + sub for sub in range(NA_SUBSTEPS)]
    band_rows = [jnp.clip(NA_ROWS * g - NA_WIN_H // 2, 0, grid_rows - NA_BAND) for g in groups]
    variants = [(g > 0).astype(jnp.int32) + (g == last_group).astype(jnp.int32) for g in groups]

    def scores(sub, c):
        k0 = pl.multiple_of(band_rows[sub] * GRID_W, kt)
        qc = q_ref[0, sub * nq:(sub + 1) * nq, c * LANES:(c + 1) * LANES].astype(jnp.float32)
        qm = jnp.concatenate([jnp.where(low, qc, 0.0), jnp.where(low, 0.0, qc)], axis=0).astype(jnp.bfloat16)
        s = lax.dot_general(k_ref[0, pl.ds(k0, nk), c * LANES:(c + 1) * LANES], qm, (((1,), (1,)), ((), ())),
                            preferred_element_type=jnp.float32)
        s = s + bias_ref[variants[sub], c]
        return s, jnp.max(jnp.max(s.reshape(nk // 8, 8, 2 * nq), axis=0), axis=0, keepdims=True)

    def weighted(sub, c, s, m):
        kb = band_rows[sub] // (kt // GRID_W)
        accs = [jnp.zeros((LANES, nq), jnp.float32) for _ in range(2)]
        for t in range(nk // kt):
            p = jnp.exp2(s[t * kt:(t + 1) * kt] - m).astype(jnp.bfloat16)
            for half in range(2):
                accs[half] = accs[half] + jnp.dot(vt_ref[half, kb + t, c * LANES:(c + 1) * LANES, :],
                                                  p[:, half * nq:(half + 1) * nq],
                                                  preferred_element_type=jnp.float32)
        halves = []
        for half in range(2):
            other = (1 - half) * HEAD_DIM
            halves.append(accs[half][half * HEAD_DIM:(half + 1) * HEAD_DIM] / accs[half][other:other + 1])
        o_ref[0, sub * nq:(sub + 1) * nq, c * LANES:(c + 1) * LANES] = (
            jnp.concatenate(halves, axis=0).T.astype(o_ref.dtype))

    units = [(sub, c) for c in range(NA_WIDTH // LANES) for sub in range(NA_SUBSTEPS)]
    ahead = NA_SUBSTEPS
    pending = [scores(*u) for u in units[:ahead]]
    for n, u in enumerate(units):
        if n + ahead < len(units):
            pending.append(scores(*units[n + ahead]))
        weighted(*u, *pending.pop(0))


def _na_attention(proj3, bias):
    nb, seq, _ = proj3.shape
    grid_rows = seq // GRID_W
    assert grid_rows % (NA_ROWS * NA_SUBSTEPS) == 0 and grid_rows >= NA_BAND
    steps = grid_rows // (NA_ROWS * NA_SUBSTEPS)
    nq = NA_ROWS * GRID_W
    return pl.pallas_call(
        functools.partial(_na_kernel, grid_rows=grid_rows),
        grid=(nb, steps),
        in_specs=[
            pl.BlockSpec((1, NA_SUBSTEPS * nq, NA_WIDTH), lambda b, g: (b, g, 0)),
            pl.BlockSpec((1, seq, NA_WIDTH), lambda b, g: (b, 0, 1)),
            pl.BlockSpec((1, seq, NA_WIDTH), lambda b, g: (b, 0, 2)),
            pl.BlockSpec(bias.shape, lambda b, g: (0, 0, 0, 0), pipeline_mode=pl.Buffered(1)),
        ],
        out_specs=pl.BlockSpec((1, NA_SUBSTEPS * nq, NA_WIDTH), lambda b, g: (b, g, 0)),
        out_shape=jax.ShapeDtypeStruct((nb, seq, NA_WIDTH), jnp.bfloat16),
        scratch_shapes=[pltpu.VMEM((2, seq // NA_KEY_TILE, NA_WIDTH, NA_KEY_TILE), jnp.bfloat16)],
        compiler_params=pltpu.CompilerParams(dimension_semantics=("parallel", "arbitrary"),
                                             vmem_limit_bytes=VMEM_LIMIT),
        name="neighbourhood_attention",
    )(proj3, proj3, proj3, bias)


def _na_bias_tables(rpb_l):
    i = np.arange(NA_ROWS)[:, None]
    a = np.arange(NA_BAND)[None, :]
    d = a - i
    start = np.stack([np.zeros_like(d), np.zeros_like(d) + i, np.zeros_like(d) + 4])
    row_ok = (a[None] >= start) & (a[None] < start + NA_WIN_H)
    row_off = np.stack([d + 7, d + 3, d - 1])
    row_off = np.clip(row_off, 0, 2 * NA_WIN_H - 2)
    cq = np.arange(GRID_W)[:, None]
    ck = np.arange(GRID_W)[None, :]
    c0 = np.clip(cq - NA_WIN_W // 2, 0, GRID_W - NA_WIN_W)
    col_ok = (ck >= c0) & (ck < c0 + NA_WIN_W)
    col_off = np.clip(ck - cq + NA_WIN_W - 1, 0, 2 * NA_WIN_W - 2)
    cols = jnp.where(col_ok.T[None, None], rpb_l.astype(jnp.float32)[:, :, col_off.T] * LOG2_E, MASKED)
    t = jnp.stack([cols[:, r] for r in row_off.reshape(-1)], axis=1)
    t = t.reshape(NA_HEADS // 2, 2, 3, NA_ROWS, NA_BAND, GRID_W, GRID_W)
    t = jnp.where(row_ok[None, None, :, :, :, None, None], t, MASKED)
    t = jnp.transpose(t, (2, 0, 4, 5, 1, 3, 6))
    return t.reshape(3, NA_HEADS // 2, NA_BAND * GRID_W, 2 * NA_ROWS * GRID_W)


def _gqa_kernel(trips_ref, q_ref, k_ref, v_ref, o_ref, vt_ref, s0_ref, s1_ref, acc0_ref, acc1_ref,
                qm0_ref, qm1_ref, stage_ref, *, seq):
    tq, tk = _gqa_q_tile(seq), GQA_K_TILE
    n_chunks = seq // tk
    s_refs = (s0_ref, s1_ref)
    acc_refs = (acc0_ref, acc1_ref)
    qm_refs = (qm0_ref, qm1_ref)
    MAX_ROWS = 8

    nbb = q_ref.shape[0]
    elems = range(nbb)

    @pl.when(pl.program_id(1) == 0)
    def _():
        ones = jnp.ones((HEAD_DIM, tk), jnp.bfloat16)
        for e in elems:
            for j in range(n_chunks):
                t = v_ref[e, j * tk:(j + 1) * tk, :].astype(jnp.float32).T
                for hk in range(GQA_KV_HEADS):
                    vt_ref[e, hk, j, :HEAD_DIM, :] = t[hk * HEAD_DIM:(hk + 1) * HEAD_DIM].astype(jnp.bfloat16)
                    vt_ref[e, hk, j, HEAD_DIM:, :] = ones

    low = _low_half((tq, LANES))
    heads = [(c, hk) for c in range(GQA_Q_WIDTH // LANES) for hk in range(GQA_KV_HEADS)]

    def masked_q(e, c, hk):
        qc = q_ref[e, :, c * LANES:(c + 1) * LANES].astype(jnp.float32)
        own = low if hk == 0 else jnp.logical_not(low)
        return jnp.where(own, qc, 0.0).astype(jnp.bfloat16)

    def scores(e, j, qm, slot, mx):
        r0 = j * tk
        s = lax.dot_general(k_ref[e, pl.ds(r0, tk), :], qm, (((1,), (1,)), ((), ())),
                            preferred_element_type=jnp.float32)
        s_refs[slot][e, pl.ds(r0, tk), :] = s
        return jnp.maximum(mx, jnp.max(s.reshape(tk // MAX_ROWS, MAX_ROWS, tq), axis=0))

    def weighted(e, j, h, m):
        r0 = j * tk
        p = jnp.exp2(s_refs[h % 2][e, pl.ds(r0, tk), :] - m).astype(jnp.bfloat16)
        part = jnp.dot(vt_ref[e, heads[h][1], j], p, preferred_element_type=jnp.float32)
        if j == 0:
            acc_refs[h % 2][e] = part
        else:
            acc_refs[h % 2][e] += part

    def finish(e, h):
        c, hk = heads[h]
        acc = acc_refs[h % 2][e]
        stage_ref[e, hk * HEAD_DIM:(hk + 1) * HEAD_DIM, :] = acc[:HEAD_DIM] / acc[HEAD_DIM:HEAD_DIM + 1]
        if hk == GQA_KV_HEADS - 1:
            o_ref[e, :, c * LANES:(c + 1) * LANES] = stage_ref[e].T.astype(o_ref.dtype)

    n_heads = len(heads)
    mx0 = tuple(jnp.full((MAX_ROWS, tq), -jnp.inf, jnp.float32) for _ in elems)
    for e in elems:
        qm_refs[0][e] = masked_q(e, *heads[0])
    m = None
    for i in range(n_heads + 1):
        def body(mx, i=i, m=m):
            mx = list(mx)
            for e in elems:
                if i >= 2:
                    finish(e, i - 2)
                if i + 1 < n_heads:
                    qm_refs[(i + 1) % 2][e] = masked_q(e, *heads[i + 1])
            qms = [qm_refs[i % 2][e] if i < n_heads else None for e in elems]
            for j in range(n_chunks):
                for e in elems:
                    if i >= 1:
                        weighted(e, j, i - 1, m[e])
                    if i < n_heads:
                        mx[e] = scores(e, j, qms[e], i % 2, mx[e])
            return tuple(mx)

        mx = lax.fori_loop(0, trips_ref[0], lambda t, mx, body=body: body(mx), mx0)
        m = [jnp.max(mx[e], axis=0, keepdims=True) for e in elems]
    for e in elems:
        finish(e, n_heads - 1)


def _gqa_q_tile(seq):
    return min(seq, GQA_MAX_Q_TILE, GQA_SCORE_BYTES // (2 * 4 * seq))


def _gqa_attention(proj3):
    nb, seq, _ = proj3.shape
    tq, tk = _gqa_q_tile(seq), GQA_K_TILE
    assert seq % tk == 0 and seq // tk <= GQA_MAX_CHUNKS and seq % tq == 0
    q_blk = 0
    k_blk = GQA_Q_WIDTH // GQA_KV_WIDTH
    trips = jnp.ones((1,), jnp.int32)
    nbb = max(1, GQA_SCORE_BYTES // (2 * 4 * seq * tq))
    while nb % nbb:
        nbb -= 1
    return pl.pallas_call(
        functools.partial(_gqa_kernel, seq=seq),
        grid=(nb // nbb, seq // tq),
        in_specs=[
            pl.BlockSpec(memory_space=pltpu.SMEM),
            pl.BlockSpec((nbb, tq, GQA_Q_WIDTH), lambda b, i: (b, i, q_blk)),
            pl.BlockSpec((nbb, seq, GQA_KV_WIDTH), lambda b, i: (b, 0, k_blk)),
            pl.BlockSpec((nbb, seq, GQA_KV_WIDTH), lambda b, i: (b, 0, k_blk + 1)),
        ],
        out_specs=pl.BlockSpec((nbb, tq, GQA_Q_WIDTH), lambda b, i: (b, i, 0)),
        out_shape=jax.ShapeDtypeStruct((nb, seq, GQA_Q_WIDTH), jnp.bfloat16),
        scratch_shapes=[
            pltpu.VMEM((nbb, GQA_KV_HEADS, seq // tk, LANES, tk), jnp.bfloat16),
            pltpu.VMEM((nbb, seq, tq), jnp.float32),
            pltpu.VMEM((nbb, seq, tq), jnp.float32),
            pltpu.VMEM((nbb, LANES, tq), jnp.float32),
            pltpu.VMEM((nbb, LANES, tq), jnp.float32),
            pltpu.VMEM((nbb, tq, LANES), jnp.bfloat16),
            pltpu.VMEM((nbb, tq, LANES), jnp.bfloat16),
            pltpu.VMEM((nbb, LANES, tq), jnp.float32),
        ],
        compiler_params=pltpu.CompilerParams(dimension_semantics=("parallel", "arbitrary"),
                                             vmem_limit_bytes=VMEM_LIMIT),
        name="gqa_attention",
    )(trips, proj3, proj3, proj3)


def _sigmoid(x):
    return 1.0 / (1.0 + jnp.exp(-x))


def _out_proj_kernel(x_ref, oa_ref, ob_ref, g_ref, scale_ref, shift_ref, gate_ref, wzg_ref, bg_ref,
                     wpa_ref, wpb_ref, wout_ref, fg_ref, o_ref, *, final):
    sub = x_ref.shape[0] // TOKEN_SUBTILES
    for t in range(TOKEN_SUBTILES):
        rows = slice(t * sub, (t + 1) * sub)
        x = x_ref[rows, :]
        hid = _modulated_norm(x, g_ref[...], scale_ref[0], shift_ref[0])
        zg = jnp.dot(hid.astype(jnp.bfloat16), wzg_ref[...], preferred_element_type=jnp.float32)
        za = zg[:, :NA_WIDTH]
        zb = zg[:, NA_WIDTH:NA_WIDTH + GQA_Q_WIDTH]
        ua = oa_ref[rows, :].astype(jnp.float32) * (za * _sigmoid(za))
        ub = ob_ref[rows, :].astype(jnp.float32) * (zb * _sigmoid(zb))
        ya = jnp.dot(ua.astype(jnp.bfloat16), wpa_ref[...], preferred_element_type=jnp.float32)
        yb = jnp.dot(ub.astype(jnp.bfloat16), wpb_ref[...], preferred_element_type=jnp.float32)
        gates = _sigmoid(zg[:, NA_WIDTH + GQA_Q_WIDTH:] + bg_ref[...])
        merged = gates[:, :D_MODEL] * ya + gates[:, D_MODEL:] * yb
        y = jnp.dot(merged.astype(jnp.bfloat16), wout_ref[...], preferred_element_type=jnp.float32)
        out = x + gate_ref[0] * y
        if final:
            ms = jnp.mean(out * out, axis=-1, keepdims=True)
            out = out * lax.rsqrt(ms + EPS) * fg_ref[...]
        o_ref[rows, :] = out


def _out_projection(x2, oa, ob, norm_g, scale, shift, gate, wzg, b_gate, wpa, wpb, wout, final_g, seq, final):
    tokens = x2.shape[0]
    tm = TOKEN_TILE
    per_seq = seq // tm
    vec = lambda i: (0, 0)
    per_batch = lambda i: (i // per_seq, 0, 0)
    return pl.pallas_call(
        functools.partial(_out_proj_kernel, final=final),
        grid=(tokens // tm,),
        in_specs=[
            pl.BlockSpec((tm, D_MODEL), lambda i: (i, 0)),
            pl.BlockSpec((tm, NA_WIDTH), lambda i: (i, 0)),
            pl.BlockSpec((tm, GQA_Q_WIDTH), lambda i: (i, 0)),
            pl.BlockSpec((1, D_MODEL), vec),
            pl.BlockSpec((1, 1, D_MODEL), per_batch),
            pl.BlockSpec((1, 1, D_MODEL), per_batch),
            pl.BlockSpec((1, 1, D_MODEL), per_batch),
            pl.BlockSpec((D_MODEL, ZG_WIDTH), vec, pipeline_mode=pl.Buffered(1)),
            pl.BlockSpec((1, 2 * D_MODEL), vec),
            pl.BlockSpec((NA_WIDTH, D_MODEL), vec, pipeline_mode=pl.Buffered(1)),
            pl.BlockSpec((GQA_Q_WIDTH, D_MODEL), vec, pipeline_mode=pl.Buffered(1)),
            pl.BlockSpec((D_MODEL, D_MODEL), vec, pipeline_mode=pl.Buffered(1)),
            pl.BlockSpec((1, D_MODEL), vec),
        ],
        out_specs=pl.BlockSpec((tm, D_MODEL), lambda i: (i, 0)),
        out_shape=jax.ShapeDtypeStruct((tokens, D_MODEL), jnp.float32),
        compiler_params=pltpu.CompilerParams(dimension_semantics=("parallel",), vmem_limit_bytes=VMEM_LIMIT),
        name="out_projection",
    )(x2, oa, ob, norm_g, scale, shift, gate, wzg, b_gate, wpa, wpb, wout, final_g)


def _rope_tables(seq):
    t = jnp.arange(seq)
    row = (t // GRID_W).astype(jnp.float32)
    col = (t % GRID_W).astype(jnp.float32)
    inv = ROPE_THETA ** (-jnp.arange(0, ROPE_AXIS_DIM, 2, dtype=jnp.float32) / ROPE_AXIS_DIM)
    ang_r, ang_c = row[:, None] * inv, col[:, None] * inv
    cos = jnp.concatenate([jnp.cos(ang_r)] * 2 + [jnp.cos(ang_c)] * 2, axis=-1)
    sin = jnp.concatenate([-jnp.sin(ang_r), jnp.sin(ang_r), -jnp.sin(ang_c), jnp.sin(ang_c)], axis=-1)
    return jnp.tile(cos, (1, 2)), jnp.tile(sin, (1, 2))


def _prepare_layer(l, norm_g, w_in, b_gate, rpb, q_norm_g, k_norm_g, w_pa, w_pb, w_out):
    bounds = np.concatenate([[0], np.cumsum(IN_SPLITS)])
    qa, ka, va, za, qb, kb, vb, zb, ga, gb = (w_in[l][:, bounds[i]:bounds[i + 1]] for i in range(10))
    bf = jnp.bfloat16

    def pair_heads(w):
        r = w.shape[0]
        w = w.reshape(r, GQA_KV_HEADS, GQA_GROUP, HEAD_DIM)
        return jnp.transpose(w, (0, 2, 1, 3)).reshape(r, GQA_Q_WIDTH)

    return dict(
        norm_g=norm_g[l][None, :],
        w_gqa=jnp.concatenate([pair_heads(qb), kb, vb], axis=1).astype(bf),
        w_na=jnp.concatenate([qa * (ATTN_SCALE * LOG2_E), ka, va], axis=1).astype(bf),
        wzg=jnp.concatenate([za, pair_heads(zb), ga, gb], axis=1).astype(bf),
        b_gate=b_gate[l][None, :],
        bias=_na_bias_tables(rpb[l]),
        qg=jnp.tile(q_norm_g[l] * (ATTN_SCALE * LOG2_E), 2)[None, :],
        kg=jnp.tile(k_norm_g[l], 2)[None, :],
        wpa=w_pa[l].astype(bf),
        wpb=jnp.transpose(w_pb[l].reshape(GQA_KV_HEADS, GQA_GROUP, HEAD_DIM, D_MODEL),
                          (1, 0, 2, 3)).reshape(GQA_Q_WIDTH, D_MODEL).astype(bf),
        wout=w_out[l].astype(bf),
    )


def _trunk(x, mod, layers, final_g):
    nb, seq, _ = x.shape
    x2 = x.reshape(nb * seq, D_MODEL)
    cos_t, sin_t = _rope_tables(seq)
    for l, p in enumerate(layers):
        shift, scale, gate = (mod[l][:, None, i * D_MODEL:(i + 1) * D_MODEL] for i in range(3))
        proj_gqa, proj_na = _in_projection(x2, p["norm_g"], scale, shift, p["w_gqa"], p["w_na"], cos_t, sin_t,
                                           p["qg"], p["kg"], seq)
        oa = _na_attention(proj_na.reshape(nb, seq, NA_PROJ_WIDTH), p["bias"]).reshape(nb * seq, NA_WIDTH)
        ob = _gqa_attention(proj_gqa.reshape(nb, seq, GQA_PROJ_WIDTH)).reshape(nb * seq, GQA_Q_WIDTH)
        x2 = _out_projection(x2, oa, ob, p["norm_g"], scale, shift, gate, p["wzg"], p["b_gate"],
                             p["wpa"], p["wpb"], p["wout"], final_g[None, :], seq, final=(l == len(layers) - 1))
    return x2.reshape(nb, seq, D_MODEL)


def kernel(x_prompt, x_sample, c_prompt, c_sample, norm_g, w_ada, b_ada, w_in, b_gate, rpb, q_norm_g, k_norm_g, w_pa, w_pb, w_out, final_g):
    layers = [_prepare_layer(l, norm_g, w_in, b_gate, rpb, q_norm_g, k_norm_g, w_pa, w_pb, w_out)
              for l in range(DEPTH)]
    n_prompt = c_prompt.shape[0]
    mod = _modulation(jnp.concatenate([c_prompt, c_sample], axis=0), w_ada, b_ada)
    y_prompt = _trunk(x_prompt, mod[:, :n_prompt], layers, final_g)
    y_sample = _trunk(x_sample, mod[:, n_prompt:], layers, final_g)
    return (y_prompt, y_sample)
```
